```python
import jax, jax.numpy as jnp
from jax import lax
import numpy as np

D_MODEL = 1024
BATCH = 16
SEQ = 2048
DEPTH = 2
DEC_BATCH = 32
DEC_SEQ = 4
PAST_LEN = 16384
PAGE_SIZE = 128

MIX_WIDTH = D_MODEL
SB_WIDTH = MIX_WIDTH // 2
SB_HEAD_DIM = 64
SB_HEADS = SB_WIDTH // SB_HEAD_DIM
SB_BIAS_INIT = -6.0
Q_BLOCK = 128
GLA_WIDTH = MIX_WIDTH - SB_WIDTH
GLA_HEADS = 4
GLA_DV = GLA_WIDTH // GLA_HEADS
GLA_DK = GLA_DV // 2
GLA_KW = GLA_HEADS * GLA_DK
GLA_GATE_RANK = 16
GLA_GATE_NORM = 16.0
GLA_CHUNK = 64
SPLITS = (SB_WIDTH, 2 * SB_WIDTH, 3 * SB_WIDTH,
          3 * SB_WIDTH + GLA_KW, 3 * SB_WIDTH + 2 * GLA_KW,
          3 * SB_WIDTH + 2 * GLA_KW + GLA_WIDTH, 3 * SB_WIDTH + 2 * GLA_KW + 2 * GLA_WIDTH)
D_IN = SPLITS[-1] + GLA_GATE_RANK
D_FF = ((8 * D_MODEL // 3 + 127) // 128) * 128
N_EXPERTS = 8
TOP_K = 2
N_DENSE = (DEPTH + 1) // 2
N_MOE = DEPTH // 2
ALPHA = (2 * DEPTH) ** 0.25
BETA_INIT = (8 * DEPTH) ** -0.25
LN_EPS = 1e-5
RMS_EPS = 1e-6

kernel_name = 'stick_breaking_gla_hybrid_step'


def layer_norm(x, g, b):
    xf = x.astype(jnp.float32)
    mu = jnp.mean(xf, axis=-1, keepdims=True)
    var = jnp.mean(jnp.square(xf - mu), axis=-1, keepdims=True)
    y = (xf - mu) * lax.rsqrt(var + LN_EPS) * g.astype(jnp.float32) + b.astype(jnp.float32)
    return y.astype(x.dtype)


def head_rms_norm(o, g):
    of = o.astype(jnp.float32)
    of = of * lax.rsqrt(jnp.mean(jnp.square(of), axis=-1, keepdims=True) + RMS_EPS)
    return (of.reshape(*o.shape[:-2], -1) * g.astype(jnp.float32)).astype(o.dtype)


def project(h, w_in, w_gate_up, b_gate):
    B, L, _ = h.shape
    p = jnp.einsum('bld,de->ble', h, w_in)
    sq, sk, sv, gq, gk, gv, gr, gdown = jnp.split(p, SPLITS, axis=-1)
    log_g = jax.nn.log_sigmoid((jnp.einsum('blr,re->ble', gdown, w_gate_up) + b_gate).astype(jnp.float32)) / GLA_GATE_NORM
    sb = lambda a: a.reshape(B, L, SB_HEADS, SB_HEAD_DIM)
    gl = lambda a, d: a.reshape(B, L, GLA_HEADS, d)
    return (sb(sq), sb(sk), sb(sv), gl(gq, GLA_DK) * (GLA_DK ** -0.5), gl(gk, GLA_DK), gl(gv, GLA_DV),
            gr, gl(log_g, GLA_DK))


def sb_attend(q, k, v, q_pos, k_pos, bias):
    z = (jnp.einsum('bqhd,bkhd->bhqk', q, k).astype(jnp.float32) * (SB_HEAD_DIM ** -0.5)
         + bias.astype(jnp.float32)[None, :, None, None])
    mask = k_pos[None, :] < q_pos[:, None]
    log_keep = jnp.where(mask, jax.nn.log_sigmoid(-z), 0.0)
    suffix = lax.cumsum(log_keep, axis=3, reverse=True) - log_keep
    w = jnp.where(mask, jnp.exp(jax.nn.log_sigmoid(z) + suffix), 0.0)
    return jnp.einsum('bhqk,bkhd->bqhd', w.astype(v.dtype), v)


def sb_prompt(q, k, v, bias):
    B, L, H, Dh = q.shape
    nb = L // Q_BLOCK
    pos = jnp.arange(L, dtype=jnp.int32)
    qb = q.reshape(B, nb, Q_BLOCK, H, Dh).swapaxes(0, 1)
    pb = pos.reshape(nb, Q_BLOCK)
    ob = lax.map(lambda a: sb_attend(a[0], k, v, a[1], pos, bias), (qb, pb))
    return ob.swapaxes(0, 1).reshape(B, L, H, Dh)


def gla_chunked(q, k, v, log_g, s0):
    B, L, H, DK = q.shape
    DV = v.shape[-1]
    C = min(GLA_CHUNK, L)
    n = -(-L // C)
    pad = n * C - L
    arrs = [a.astype(jnp.float32) for a in (q, k, v, log_g)]
    if pad:
        arrs = [jnp.pad(a, ((0, 0), (0, pad), (0, 0), (0, 0))) for a in arrs]
    chunks = tuple(a.reshape(B, n, C, H, a.shape[-1]).swapaxes(0, 1) for a in arrs)
    tri = jnp.tril(jnp.ones((C, C), dtype=bool))

    def step(S, inp):
        qc, kc, vc, gc = inp
        b = lax.cumsum(gc, axis=1)
        diff = b[:, :, None] - b[:, None, :]
        decay = jnp.exp(jnp.where(tri[None, :, :, None, None], diff, -jnp.inf))
        scores = jnp.einsum('bthk,btshk,bshk->bhts', qc, decay, kc)
        o = (jnp.einsum('bhts,bshv->bthv', scores, vc)
             + jnp.einsum('bthk,bhkv->bthv', qc * jnp.exp(b), S))
        b_last = b[:, -1]
        S = (jnp.exp(b_last)[..., None] * S
             + jnp.einsum('bshk,bshv->bhkv', kc * jnp.exp(b_last[:, None] - b), vc))
        return S, o

    S, o = lax.scan(step, s0.astype(jnp.float32), chunks)
    o = o.swapaxes(0, 1).reshape(B, n * C, H, DV)[:, :L]
    return o.astype(v.dtype), S.astype(s0.dtype)


def merge_heads(o_sb, o_gla, r, sb_g, gla_g, w_o):
    a = head_rms_norm(o_sb, sb_g)
    b = head_rms_norm(o_gla, gla_g) * jax.nn.silu(r)
    return jnp.einsum('ble,ed->bld', jnp.concatenate([a, b], axis=-1), w_o)


def swiglu(x, wg, wu, wd):
    h = jax.nn.silu(jnp.einsum('bld,df->blf', x, wg)) * jnp.einsum('bld,df->blf', x, wu)
    return jnp.einsum('blf,fd->bld', h, wd)


def moe_swiglu(x, w_router, b_router, wg, wu, wd):
    logits = (jnp.einsum('bld,de->ble', x, w_router) + b_router).astype(jnp.float32)
    top_v, top_i = lax.top_k(logits, TOP_K)
    gates = jax.nn.softmax(top_v, axis=-1)
    dense_gate = jnp.sum(jax.nn.one_hot(top_i, N_EXPERTS, dtype=jnp.float32) * gates[..., None], axis=-2)
    y = jnp.zeros_like(x)
    for e in range(N_EXPERTS):
        y = y + dense_gate[..., e:e + 1].astype(x.dtype) * swiglu(x, wg[e], wu[e], wd[e])
    return y


def setup_inputs(seed: int = 0) -> dict:
    key = jax.random.key(seed)
    ks = jax.random.split(key, 26)
    f32 = jnp.float32
    n_pages = PAST_LEN // PAGE_SIZE
    n_pool = (DEC_BATCH * n_pages * 5) // 4
    nrm = lambda k, shape, s: jax.random.normal(k, shape, f32) * s
    page_table = jax.random.permutation(ks[5], n_pool)[:DEC_BATCH * n_pages]
    return {
        'x_prompt': nrm(ks[0], (BATCH, SEQ, D_MODEL), 1.0),
        'x_sample': nrm(ks[1], (DEC_BATCH, DEC_SEQ, D_MODEL), 1.0),
        'cache_k': nrm(ks[2], (DEPTH, n_pool, PAGE_SIZE, SB_HEADS, SB_HEAD_DIM), 1.0),
        'cache_v': nrm(ks[3], (DEPTH, n_pool, PAGE_SIZE, SB_HEADS, SB_HEAD_DIM), 1.0),
        'state_gla': nrm(ks[4], (DEPTH, DEC_BATCH, GLA_HEADS, GLA_DK, GLA_DV), 0.5),
        'page_table': page_table.reshape(DEC_BATCH, n_pages).astype(jnp.int32),
        'w_in': nrm(ks[6], (DEPTH, D_MODEL, D_IN), D_MODEL ** -0.5),
        'w_gate_up': nrm(ks[7], (DEPTH, GLA_GATE_RANK, GLA_KW), GLA_GATE_RANK ** -0.5),
        'b_gate': nrm(ks[8], (DEPTH, GLA_KW), 0.1),
        'sb_bias': SB_BIAS_INIT + nrm(ks[24], (DEPTH, SB_HEADS), 0.3),
        'sb_norm_g': 1.0 + nrm(ks[9], (DEPTH, SB_WIDTH), 0.02),
        'gla_norm_g': 1.0 + nrm(ks[10], (DEPTH, GLA_WIDTH), 0.02),
        'w_o': nrm(ks[11], (DEPTH, MIX_WIDTH, D_MODEL), MIX_WIDTH ** -0.5 * BETA_INIT),
        'ln1_g': 1.0 + nrm(ks[12], (DEPTH, D_MODEL), 0.02),
        'ln1_b': nrm(ks[13], (DEPTH, D_MODEL), 0.02),
        'ln2_g': 1.0 + nrm(ks[14], (DEPTH, D_MODEL), 0.02),
        'ln2_b': nrm(ks[15], (DEPTH, D_MODEL), 0.02),
        'ffn_w_gate': nrm(ks[16], (N_DENSE, D_MODEL, D_FF), D_MODEL ** -0.5),
        'ffn_w_up': nrm(ks[17], (N_DENSE, D_MODEL, D_FF), D_MODEL ** -0.5),
        'ffn_w_down': nrm(ks[18], (N_DENSE, D_FF, D_MODEL), D_FF ** -0.5 * BETA_INIT),
        'router_w': nrm(ks[19], (N_MOE, D_MODEL, N_EXPERTS), D_MODEL ** -0.5),
        'router_b': nrm(ks[20], (N_MOE, N_EXPERTS), 0.01),
        'moe_w_gate': nrm(ks[21], (N_MOE, N_EXPERTS, D_MODEL, D_FF), D_MODEL ** -0.5),
        'moe_w_up': nrm(ks[22], (N_MOE, N_EXPERTS, D_MODEL, D_FF), D_MODEL ** -0.5),
        'moe_w_down': nrm(ks[23], (N_MOE, N_EXPERTS, D_FF, D_MODEL), D_FF ** -0.5 * BETA_INIT),
    }


def reference(x_prompt, x_sample, cache_k, cache_v, state_gla, page_table, w_in, w_gate_up, b_gate, sb_bias,
              sb_norm_g, gla_norm_g, w_o, ln1_g, ln1_b, ln2_g, ln2_b, ffn_w_gate, ffn_w_up, ffn_w_down,
              router_w, router_b, moe_w_gate, moe_w_up, moe_w_down):
    Bp = x_prompt.shape[0]
    Bd, Ld = x_sample.shape[:2]
    n_pages = page_table.shape[1]
    past_len = n_pages * PAGE_SIZE
    k_pos_s = jnp.arange(past_len + Ld, dtype=jnp.int32)
    q_pos_s = past_len + jnp.arange(Ld, dtype=jnp.int32)
    xp, xs = x_prompt, x_sample
    kp_rows, vp_rows, sp_states, ks_rows, vs_rows, ss_states = [], [], [], [], [], []
    for l in range(DEPTH):
        sq, sk, sv, gq, gk, gv, gr, lg = project(xp, w_in[l], w_gate_up[l], b_gate[l])
        o_sb = sb_prompt(sq, sk, sv, sb_bias[l])
        o_gla, s_p = gla_chunked(gq, gk, gv, lg, jnp.zeros((Bp, GLA_HEADS, GLA_DK, GLA_DV), jnp.float32))
        xp = layer_norm(ALPHA * xp + merge_heads(o_sb, o_gla, gr, sb_norm_g[l], gla_norm_g[l], w_o[l]),
                        ln1_g[l], ln1_b[l])
        kp_rows.append(sk)
        vp_rows.append(sv)
        sp_states.append(s_p)
        sq, sk, sv, gq, gk, gv, gr, lg = project(xs, w_in[l], w_gate_up[l], b_gate[l])
        past_k = cache_k[l][page_table].reshape(Bd, past_len, SB_HEADS, SB_HEAD_DIM)
        past_v = cache_v[l][page_table].reshape(Bd, past_len, SB_HEADS, SB_HEAD_DIM)
        o_sb = sb_attend(sq, jnp.concatenate([past_k, sk], axis=1), jnp.concatenate([past_v, sv], axis=1),
                         q_pos_s, k_pos_s, sb_bias[l])
        o_gla, s_s = gla_chunked(gq, gk, gv, lg, state_gla[l])
        xs = layer_norm(ALPHA * xs + merge_heads(o_sb, o_gla, gr, sb_norm_g[l], gla_norm_g[l], w_o[l]),
                        ln1_g[l], ln1_b[l])
        ks_rows.append(sk)
        vs_rows.append(sv)
        ss_states.append(s_s)
        i = l // 2
        if l % 2 == 0:
            fp = swiglu(xp, ffn_w_gate[i], ffn_w_up[i], ffn_w_down[i])
            fs = swiglu(xs, ffn_w_gate[i], ffn_w_up[i], ffn_w_down[i])
        else:
            fp = moe_swiglu(xp, router_w[i], router_b[i], moe_w_gate[i], moe_w_up[i], moe_w_down[i])
            fs = moe_swiglu(xs, router_w[i], router_b[i], moe_w_gate[i], moe_w_up[i], moe_w_down[i])
        xp = layer_norm(ALPHA * xp + fp, ln2_g[l], ln2_b[l])
        xs = layer_norm(ALPHA * xs + fs, ln2_g[l], ln2_b[l])
    new_k_prompt = jnp.stack(kp_rows, axis=0)
    new_v_prompt = jnp.stack(vp_rows, axis=0)
    gla_state_prompt = jnp.stack(sp_states, axis=0)
    new_k_sample = jnp.stack(ks_rows, axis=0)
    new_v_sample = jnp.stack(vs_rows, axis=0)
    gla_state_sample = jnp.stack(ss_states, axis=0)
    return (xp, xs, new_k_prompt, new_v_prompt, gla_state_prompt, new_k_sample, new_v_sample, gla_state_sample)
```

```python
import functools

import jax
import jax.numpy as jnp
from jax import lax
from jax.experimental import pallas as pl
from jax.experimental.pallas import tpu as pltpu

F32 = jnp.float32
BF16 = jnp.bfloat16

LN_EPS = 1e-5
RMS_EPS = 1e-6
GLA_GATE_NORM = 16.0
GLA_GATE_PAD = 128
ROUTER_PAD = 128
TOP_K = 2
VMEM_LIMIT = 56 * 1024 * 1024

ROW_TILE = 512
SB_TILE = 512
SB_SUB = 128
SB_KEY_CHUNK = 256
GLA_CHUNK = 64
GLA_CHUNKS_PER_STEP = 4
GLA_SAMPLE_CHUNK = 16
FF_CHUNK = 256
DEC_PAGES_PER_STEP = 8


def _tile(n, pref):
    t = min(n, pref)
    while n % t:
        t -= 8
    assert t > 0 and (t % 8 == 0 or t == n)
    return t


def _cparams(sem):
    return pltpu.CompilerParams(dimension_semantics=sem, vmem_limit_bytes=VMEM_LIMIT)


def _softplus(z):
    return jnp.maximum(z, 0.0) + jnp.log(1.0 + jnp.exp(-jnp.abs(z)))


def _split(x):
    hi = x.astype(BF16)
    lo = (x - hi.astype(F32)).astype(BF16)
    return hi, lo


def _dot(a, b):
    return jnp.dot(a, b, preferred_element_type=F32)


def _dot_nt(a, b):
    return lax.dot_general(a, b, (((1,), (1,)), ((), ())), preferred_element_type=F32)


def _dot_tn(a, b):
    return lax.dot_general(a, b, (((0,), (0,)), ((), ())), preferred_element_type=F32)


def _layer_norm(x, g, b):
    mu = jnp.mean(x, axis=-1, keepdims=True)
    xc = x - mu
    var = jnp.mean(xc * xc, axis=-1, keepdims=True)
    return xc * lax.rsqrt(var + LN_EPS) * g + b


def _silu(x):
    return x / (1.0 + jnp.exp(-x))


def _in_proj_kernel(x_ref, w_ref, wkvt_ref, wgd_ref, wup_ref, bg_ref,
                    q_ref, k_ref, v_ref, gq_ref, gk_ref, lg_ref, gv_ref, gr_ref,
                    *, H, Dh, GH, DK, GW, kv_transposed):
    xb = x_ref[...].astype(BF16)
    SW = H * Dh
    KW = GH * DK

    def seg(lo, n):
        return _dot(xb, w_ref[:, lo:lo + n])

    if kv_transposed:
        k_ref[...] = _dot_nt(wkvt_ref[0:SW, :], xb)
        v_ref[...] = _dot_nt(wkvt_ref[SW:2 * SW, :], xb)
    else:
        k_ref[...] = _dot_nt(xb, wkvt_ref[0:SW, :])
        v_ref[...] = _dot_nt(xb, wkvt_ref[SW:2 * SW, :])
    pq = seg(0, SW) * (Dh ** -0.5)
    for h in range(H):
        q_ref[h] = pq[:, h * Dh:(h + 1) * Dh].astype(BF16)
    o = SW
    gq = seg(o, KW) * (DK ** -0.5)
    gk = seg(o + KW, KW)
    gv_ref[...] = seg(o + 2 * KW, GW)
    gr_ref[...] = seg(o + 2 * KW + GW, GW)
    gd = _dot(xb, wgd_ref[...]).astype(BF16)
    u = _dot(gd, wup_ref[...]) + bg_ref[...]
    lg = -_softplus(-u) * (1.0 / GLA_GATE_NORM)
    for h in range(GH):
        sl = slice(h * DK, (h + 1) * DK)
        gq_ref[h] = gq[:, sl]
        gk_ref[h] = gk[:, sl]
        lg_ref[h] = lg[:, sl]


def _in_proj(x, w_main, w_kvt, w_gd, w_up, b_gate, dims, seq_len=None):
    H, Dh, GH, DK, DV = dims
    M, D = x.shape
    SW, KW, GW = H * Dh, GH * DK, GH * DV
    tm = _tile(seq_len or M, ROW_TILE)
    row = lambda n: pl.BlockSpec((tm, n), lambda i: (i, 0))
    hm = lambda nh, n: pl.BlockSpec((nh, tm, n), lambda i: (0, i, 0))
    full = lambda a: pl.BlockSpec(a.shape, lambda i: (0,) * a.ndim)
    if seq_len:
        per_seq = seq_len // tm
        kv_shape = jax.ShapeDtypeStruct((M // seq_len, SW, seq_len), F32)
        kv_spec = pl.BlockSpec((None, SW, tm), lambda i: (i // per_seq, 0, i % per_seq))
    else:
        kv_shape, kv_spec = jax.ShapeDtypeStruct((M, SW), F32), row(SW)
    out_shape = (
        jax.ShapeDtypeStruct((H, M, Dh), BF16), kv_shape, kv_shape,
        jax.ShapeDtypeStruct((GH, M, DK), F32), jax.ShapeDtypeStruct((GH, M, DK), F32),
        jax.ShapeDtypeStruct((GH, M, DK), F32),
        jax.ShapeDtypeStruct((M, GW), F32), jax.ShapeDtypeStruct((M, GW), F32),
    )
    out_specs = (hm(H, Dh), kv_spec, kv_spec, hm(GH, DK), hm(GH, DK), hm(GH, DK), row(GW), row(GW))
    return pl.pallas_call(
        functools.partial(_in_proj_kernel, H=H, Dh=Dh, GH=GH, DK=DK, GW=GW, kv_transposed=bool(seq_len)),
        out_shape=out_shape,
        grid=(M // tm,),
        in_specs=[row(D), full(w_main), full(w_kvt), full(w_gd), full(w_up), full(b_gate)],
        out_specs=out_specs,
        compiler_params=_cparams(("parallel",)),
        name="in_proj",
    )(x, w_main, w_kvt, w_gd, w_up, b_gate)


def _sb_block(q, kt, vt, bias, u_incl, R, acc, mask):
    z = _dot(q, kt) + bias
    sp = _softplus(z)
    if mask is not None:
        sp = jnp.where(mask, sp, 0.0)
    hi, lo = _split(sp)
    S = _dot(hi, u_incl) + _dot(lo, u_incl)
    logw = z - S - R
    if mask is not None:
        logw = jnp.where(mask, logw, -1e30)
    w = jnp.exp(logw)
    acc = acc + _dot_nt(w.astype(BF16), vt)
    R = R + S[:, 0:1]
    return R, acc


def _sb_prompt_kernel(qi_ref, kj_ref, bias_ref, q_ref, kt_ref, vt_ref, g_ref, u_ref, o_ref, r_scr, acc_scr,
                      *, H, Dh, T, SUB, KC):
    p = pl.program_id(1)
    i, j = qi_ref[p], kj_ref[p]
    nsub = T // SUB

    def run(diag):
        def head(h, carry):
            bias = bias_ref[h]
            hs = pl.ds(pl.multiple_of(h * Dh, Dh), Dh)
            for r in range(nsub):
                rs = slice(r * SUB, (r + 1) * SUB)
                q = q_ref[h, rs, :]
                if diag:
                    R, acc = jnp.zeros((SUB, 1), F32), jnp.zeros((SUB, Dh), F32)
                else:
                    R, acc = r_scr[h, rs, :], acc_scr[h, rs, :]
                c1 = (r + 1) * SUB if diag else T
                while c1 > 0:
                    c0 = ((c1 - 1) // KC) * KC
                    mask = None
                    if diag and c1 > r * SUB:
                        rows = lax.broadcasted_iota(jnp.int32, (SUB, c1 - c0), 0) + r * SUB
                        cols = lax.broadcasted_iota(jnp.int32, (SUB, c1 - c0), 1) + c0
                        mask = cols < rows
                    R, acc = _sb_block(q, kt_ref[hs, c0:c1].astype(BF16), vt_ref[hs, c0:c1].astype(BF16), bias,
                                       u_ref[0:c1 - c0, 0:c1 - c0], R, acc, mask)
                    c1 = c0
                r_scr[h, rs, :] = R
                acc_scr[h, rs, :] = acc
            return carry

        lax.fori_loop(0, H, head, 0)

    @pl.when(i == j)
    def _():
        run(True)

    @pl.when(i != j)
    def _():
        run(False)

    @pl.when(j == 0)
    def _():
        def normed(h):
            o = acc_scr[h]
            ms = jnp.mean(o * o, axis=-1, keepdims=True)
            return o * lax.rsqrt(ms + RMS_EPS) * g_ref[h]

        for hp in range(H // 2):
            o_ref[:, 2 * hp * Dh:(2 * hp + 2) * Dh] = jnp.concatenate(
                [normed(2 * hp), normed(2 * hp + 1)], axis=-1).astype(BF16)


def _sb_prompt(q_hm, kt, vt, bias, g_hm):
    H, M, Dh = q_hm.shape
    B, SW, L = kt.shape
    T = _tile(L, SB_TILE)
    SUB = _tile(T, SB_SUB)
    KC = _tile(T, SB_KEY_CHUNK)
    nq = L // T
    pairs = [(i, j) for i in range(nq) for j in range(i, -1, -1)]
    qi = jnp.array([p[0] for p in pairs], jnp.int32)
    kj = jnp.array([p[1] for p in pairs], jnp.int32)
    u = (jnp.arange(KC)[:, None] >= jnp.arange(KC)[None, :]).astype(BF16)
    kv_spec = pl.BlockSpec((None, SW, T), lambda b, p, qi, kj: (b, 0, kj[p]))
    grid_spec = pltpu.PrefetchScalarGridSpec(
        num_scalar_prefetch=2,
        grid=(B, len(pairs)),
        in_specs=[
            pl.BlockSpec(memory_space=pltpu.SMEM),
            pl.BlockSpec((H, T, Dh), lambda b, p, qi, kj: (0, b * nq + qi[p], 0)),
            kv_spec, kv_spec,
            pl.BlockSpec((H, 1, Dh), lambda b, p, qi, kj: (0, 0, 0)),
            pl.BlockSpec((KC, KC), lambda b, p, qi, kj: (0, 0)),
        ],
        out_specs=pl.BlockSpec((T, SW), lambda b, p, qi, kj: (b * nq + qi[p], 0)),
        scratch_shapes=[pltpu.VMEM((H, T, 1), F32), pltpu.VMEM((H, T, Dh), F32)],
    )
    return pl.pallas_call(
        functools.partial(_sb_prompt_kernel, H=H, Dh=Dh, T=T, SUB=SUB, KC=KC),
        out_shape=jax.ShapeDtypeStruct((M, SW), BF16),
        grid_spec=grid_spec,
        compiler_params=_cparams(("parallel", "arbitrary")),
        name="sb_prompt",
    )(qi, kj, bias, q_hm, kt, vt, g_hm, u)


def _sb_decode_kernel(pt_ref, qbd_ref, kn_ref, vn_ref, bias_ref, g_ref, own_ref, u_ref, *rest, G, Ld, H, Dh):
    del pt_ref
    k_refs, v_refs = rest[:G], rest[G:2 * G]
    o_ref, acc_scr, r_scr = rest[2 * G:]
    s = pl.program_id(1)
    HQ = Ld * H
    P = u_ref.shape[0]
    q = qbd_ref[...]
    bias = bias_ref[...]
    u = u_ref[...]

    @pl.when(s == 0)
    def _():
        rows = lax.broadcasted_iota(jnp.int32, (HQ, P), 0)
        cols = lax.broadcasted_iota(jnp.int32, (HQ, P), 1)
        mask = rows >= cols * H + H
        R, acc = _sb_block(q, kn_ref[...], vn_ref[...], bias, u,
                           jnp.zeros((HQ, 1), F32), jnp.zeros((HQ, H * Dh), F32), mask)
        r_scr[...] = R
        acc_scr[...] = acc

    R = r_scr[...]
    acc = acc_scr[...]
    for i in range(G):
        R, acc = _sb_block(q, k_refs[i][...].astype(BF16), v_refs[i][...].astype(BF16), bias, u, R, acc, None)
    r_scr[...] = R
    acc_scr[...] = acc

    @pl.when(s == pl.num_programs(1) - 1)
    def _():
        o = acc * own_ref[...]
        ms = jnp.sum(o * o, axis=-1, keepdims=True) * (1.0 / Dh)
        o = o * lax.rsqrt(ms + RMS_EPS)
        g = g_ref[...]
        for t in range(Ld):
            o_ref[t:t + 1, :] = jnp.sum(o[t * H:(t + 1) * H, :], axis=0, keepdims=True) * g


def _sb_decode(layer, qbd, kt_new, vt_new, bias_col, g, cache_kt, cache_vt, page_table, Ld, H, Dh):
    Bd, HQ, SW = qbd.shape
    NP = page_table.shape[1]
    P = cache_kt.shape[3]
    G = DEC_PAGES_PER_STEP
    while NP % G:
        G //= 2
    u = (jnp.arange(P)[:, None] >= jnp.arange(P)[None, :]).astype(BF16)
    own = (jnp.arange(HQ)[:, None] % H == jnp.arange(SW)[None, :] // Dh).astype(F32)

    def page_spec(i):
        return pl.BlockSpec((None, None, SW, P),
                            lambda b, s, pt: (layer, pt[b, NP - 1 - (s * G + i)], 0, 0))

    per_b = lambda r, c: pl.BlockSpec((None, r, c), lambda b, s, pt: (b, 0, 0))
    const = lambda a: pl.BlockSpec(a.shape, lambda b, s, pt: (0,) * a.ndim)
    grid_spec = pltpu.PrefetchScalarGridSpec(
        num_scalar_prefetch=1,
        grid=(Bd, NP // G),
        in_specs=[per_b(HQ, SW), per_b(SW, P), per_b(SW, P), const(bias_col), const(g), const(own), const(u)]
        + [page_spec(i) for i in range(G)] + [page_spec(i) for i in range(G)],
        out_specs=per_b(Ld, SW),
        scratch_shapes=[pltpu.VMEM((HQ, SW), F32), pltpu.VMEM((HQ, 1), F32)],
    )
    return pl.pallas_call(
        functools.partial(_sb_decode_kernel, G=G, Ld=Ld, H=H, Dh=Dh),
        out_shape=jax.ShapeDtypeStruct((Bd, Ld, SW), F32),
        grid_spec=grid_spec,
        compiler_params=_cparams(("parallel", "arbitrary")),
        name="sb_decode",
    )(page_table, qbd, kt_new, vt_new, bias_col, g, own, u, *([cache_kt] * G), *([cache_vt] * G))


def _gla_kernel(q_ref, k_ref, lg_ref, v_ref, r_ref, s0_ref, g_ref, ltri_ref, o_ref, sout_ref, st_scr, *, C, nC):
    c = pl.program_id(2)

    @pl.when(c == 0)
    def _():
        st_scr[...] = s0_ref[...]

    ltri = ltri_ref[...]
    rows = lax.broadcasted_iota(jnp.int32, (C, C), 0)
    cols = lax.broadcasted_iota(jnp.int32, (C, C), 1)
    causal = cols <= rows
    g = g_ref[...]
    for ci in range(nC):
        sl = slice(ci * C, (ci + 1) * C)
        q, k, lg = q_ref[sl, :], k_ref[sl, :], lg_ref[sl, :]
        vb = v_ref[sl, :].astype(BF16)
        hi, lo = _split(lg)
        b = _dot(ltri, hi) + _dot(ltri, lo)
        b_last = b[C - 1:C, :]
        a = (q * jnp.exp(b)).astype(BF16)
        kd = (k * jnp.exp(-b)).astype(BF16)
        k2 = (k * jnp.exp(b_last - b)).astype(BF16)
        sc = jnp.where(causal, _dot_nt(a, kd), 0.0)
        st = st_scr[...]
        o = _dot(sc.astype(BF16), vb) + _dot_nt(a, st.astype(BF16))
        st_scr[...] = st * jnp.exp(b_last) + _dot_tn(vb, k2)
        ms = jnp.mean(o * o, axis=-1, keepdims=True)
        o_ref[sl, :] = (o * lax.rsqrt(ms + RMS_EPS) * g * _silu(r_ref[sl, :])).astype(BF16)

    @pl.when(c == pl.num_programs(2) - 1)
    def _():
        sout_ref[...] = st_scr[...]


def _gla(gq, gk, lg, gv, gr, s0t, g, B, L, C, nC):
    GH, M, DK = gq.shape
    DV = gv.shape[1] // GH
    blk = C * nC
    nblk = L // blk
    ltri = (jnp.arange(C)[:, None] >= jnp.arange(C)[None, :]).astype(BF16)
    hm = pl.BlockSpec((None, blk, DK), lambda b, h, c: (h, b * nblk + c, 0))
    tokm = pl.BlockSpec((blk, DV), lambda b, h, c: (b * nblk + c, h))
    state = pl.BlockSpec((None, None, DV, DK), lambda b, h, c: (b, h, 0, 0))
    return pl.pallas_call(
        functools.partial(_gla_kernel, C=C, nC=nC),
        out_shape=(jax.ShapeDtypeStruct((M, GH * DV), BF16), jax.ShapeDtypeStruct((B, GH, DV, DK), F32)),
        grid=(B, GH, nblk),
        in_specs=[hm, hm, hm, tokm, tokm, state,
                  pl.BlockSpec((1, DV), lambda b, h, c: (0, h)),
                  pl.BlockSpec((C, C), lambda b, h, c: (0, 0))],
        out_specs=(tokm, state),
        scratch_shapes=[pltpu.VMEM((DV, DK), F32)],
        compiler_params=_cparams(("parallel", "parallel", "arbitrary")),
        name="gla",
    )(gq, gk, lg, gv, gr, s0t, g, ltri)


def _merge_ln_kernel(a_ref, b_ref, x_ref, wa_ref, wb_ref, g_ref, beta_ref, o_ref, *, alpha):
    y = _dot(a_ref[...], wa_ref[...]) + _dot(b_ref[...], wb_ref[...])
    o_ref[...] = _layer_norm(alpha * x_ref[...] + y, g_ref[...], beta_ref[...])


def _merge_ln(a, b, x, wa, wb, g, beta, alpha):
    M, D = x.shape
    tm = _tile(M, ROW_TILE)
    row = lambda n: pl.BlockSpec((tm, n), lambda i: (i, 0))
    full = lambda t: pl.BlockSpec(t.shape, lambda i: (0,) * t.ndim)
    return pl.pallas_call(
        functools.partial(_merge_ln_kernel, alpha=alpha),
        out_shape=jax.ShapeDtypeStruct((M, D), F32),
        grid=(M // tm,),
        in_specs=[row(a.shape[1]), row(b.shape[1]), row(D), full(wa), full(wb), full(g), full(beta)],
        out_specs=row(D),
        compiler_params=_cparams(("parallel",)),
        name="merge_ln",
    )(a, b, x, wa, wb, g, beta)


def _ffn_kernel(te_ref, x_ref, wg_ref, wu_ref, wd_ref, *rest, nf, n_tiles, alpha, fuse_ln):
    if fuse_ln:
        g_ref, beta_ref, o_ref, xb_scr, acc_scr = rest
    else:
        o_ref, xb_scr, acc_scr = rest
    i = pl.program_id(0)

    @pl.when(i < te_ref[n_tiles])
    def _():
        xb_scr[...] = x_ref[...].astype(BF16)
        acc_scr[...] = jnp.zeros_like(acc_scr)

        def body(f, carry):
            xb = xb_scr[...]
            hg = _dot(xb, wg_ref[f])
            hu = _dot(xb, wu_ref[f])
            acc_scr[...] += _dot((_silu(hg) * hu).astype(BF16), wd_ref[f])
            return carry

        lax.fori_loop(0, nf, body, 0)
        if fuse_ln:
            o_ref[...] = _layer_norm(alpha * x_ref[...] + acc_scr[...], g_ref[...], beta_ref[...])
        else:
            o_ref[...] = acc_scr[...]

    @pl.when(i >= te_ref[n_tiles])
    def _():
        o_ref[...] = jnp.zeros_like(o_ref)


def _ffn(x, te, wg, wu, wd, tm, ln=None, alpha=1.0):
    R, D = x.shape
    _, nf, _, tf = wg.shape
    n_tiles = R // tm
    row = pl.BlockSpec((tm, D), lambda i, te: (i, 0))
    w_in = pl.BlockSpec((None, nf, D, tf), lambda i, te: (te[i], 0, 0, 0))
    w_out = pl.BlockSpec((None, nf, tf, D), lambda i, te: (te[i], 0, 0, 0))
    vec = pl.BlockSpec((1, D), lambda i, te: (0, 0))
    in_specs = [row, w_in, w_in, w_out] + ([vec, vec] if ln else [])
    grid_spec = pltpu.PrefetchScalarGridSpec(
        num_scalar_prefetch=1, grid=(n_tiles,), in_specs=in_specs, out_specs=row,
        scratch_shapes=[pltpu.VMEM((tm, D), BF16), pltpu.VMEM((tm, D), F32)])
    return pl.pallas_call(
        functools.partial(_ffn_kernel, nf=nf, n_tiles=n_tiles, alpha=alpha, fuse_ln=bool(ln)),
        out_shape=jax.ShapeDtypeStruct((R, D), F32),
        grid_spec=grid_spec,
        compiler_params=_cparams(("arbitrary",)),
        name="ffn_ln" if ln else "ffn_experts",
    )(te, x, wg, wu, wd, *(ln or ()))


def _router_kernel(x_ref, wh_ref, wl_ref, b_ref, c0_ref, lstrict_ref, slab_ref, cnt_ref, carry_scr, *, E):
    i = pl.program_id(0)

    @pl.when(i == 0)
    def _():
        carry_scr[...] = c0_ref[...]

    xh, xl = _split(x_ref[...])
    wh, wl = wh_ref[...], wl_ref[...]
    logits = _dot(xh, wh) + (_dot(xl, wh) + _dot(xh, wl)) + b_ref[...]
    tm, W = logits.shape
    lane = lax.broadcasted_iota(jnp.int32, (tm, W), 1).astype(F32)
    neg = -jnp.inf
    lg1 = jnp.where(lane < E, logits, neg)
    m1 = jnp.max(lg1, axis=-1, keepdims=True)
    i1 = jnp.min(jnp.where(lg1 == m1, lane, float(W)), axis=-1, keepdims=True)
    lg2 = jnp.where(lane == i1, neg, lg1)
    m2 = jnp.max(lg2, axis=-1, keepdims=True)
    i2 = jnp.min(jnp.where(lg2 == m2, lane, float(W)), axis=-1, keepdims=True)
    e2 = jnp.exp(m2 - m1)
    g1 = 1.0 / (1.0 + e2)
    g2 = e2 / (1.0 + e2)
    oh = jnp.where((lane == i1) | (lane == i2), 1.0, 0.0)
    before = _dot(lstrict_ref[...], oh.astype(BF16)) + carry_scr[...]
    r1 = jnp.sum(jnp.where(lane == i1, before, 0.0), axis=-1, keepdims=True)
    r2 = jnp.sum(jnp.where(lane == i2, before, 0.0), axis=-1, keepdims=True)
    carry_scr[...] += jnp.sum(oh, axis=0, keepdims=True)
    slab = jnp.zeros((tm, W), F32)
    for col, val in enumerate((i1, i2, g1, g2, r1, r2)):
        slab = jnp.where(lane == col, val, slab)
    slab_ref[...] = slab
    cnt_ref[...] = carry_scr[...]


def _router(x, wh, wl, b, c0, E):
    M, D = x.shape
    tm = _tile(M, ROW_TILE)
    W = wh.shape[1]
    lstrict = (jnp.arange(tm)[:, None] > jnp.arange(tm)[None, :]).astype(BF16)
    full = lambda t: pl.BlockSpec(t.shape, lambda i: (0,) * t.ndim)
    return pl.pallas_call(
        functools.partial(_router_kernel, E=E),
        out_shape=(jax.ShapeDtypeStruct((M, W), F32), jax.ShapeDtypeStruct((1, W), F32)),
        grid=(M // tm,),
        in_specs=[pl.BlockSpec((tm, D), lambda i: (i, 0)), full(wh), full(wl), full(b), full(c0), full(lstrict)],
        out_specs=(pl.BlockSpec((tm, W), lambda i: (i, 0)), pl.BlockSpec((1, W), lambda i: (0, 0))),
        scratch_shapes=[pltpu.VMEM((1, W), F32)],
        compiler_params=_cparams(("arbitrary",)),
        name="router",
    )(x, wh, wl, b, c0, lstrict)


def _row_copy(src, s, dst, d, sem):
    return pltpu.make_async_copy(src.at[pl.ds(s, 1)], dst.at[pl.ds(d, 1)], sem)


def _dispatch_kernel(dest_ref, x_hbm, xs_in, xs_out, sem, *, tm):
    del xs_in
    i = pl.program_id(0)

    def start(r, carry):
        for j in range(TOP_K):
            _row_copy(x_hbm, i * tm + r, xs_out, dest_ref[0, 0, TOP_K * r + j], sem).start()
        return carry

    def wait(r, carry):
        for j in range(TOP_K):
            _row_copy(x_hbm, 0, xs_out, 0, sem).wait()
        return carry

    lax.fori_loop(0, tm, start, 0)
    lax.fori_loop(0, tm, wait, 0)


def _dispatch(x, dest3, xs):
    M, _ = x.shape
    tm = dest3.shape[2] // TOP_K
    return pl.pallas_call(
        functools.partial(_dispatch_kernel, tm=tm),
        out_shape=jax.ShapeDtypeStruct(xs.shape, xs.dtype),
        grid=(M // tm,),
        in_specs=[pl.BlockSpec((1, 1, TOP_K * tm), lambda i: (i, 0, 0), memory_space=pltpu.SMEM),
                  pl.BlockSpec(memory_space=pl.ANY), pl.BlockSpec(memory_space=pl.ANY)],
        out_specs=pl.BlockSpec(memory_space=pl.ANY),
        scratch_shapes=[pltpu.SemaphoreType.DMA],
        input_output_aliases={2: 0},
        compiler_params=_cparams(("arbitrary",)),
        name="moe_dispatch",
    )(dest3, x, xs)


def _combine_kernel(dest_ref, slab_ref, x_ref, g_ref, beta_ref, ys_hbm, o_ref, buf, sem, *, tm, alpha):
    def start(r, carry):
        for j in range(TOP_K):
            _row_copy(ys_hbm, dest_ref[0, 0, TOP_K * r + j], buf.at[j], r, sem).start()
        return carry

    def wait(r, carry):
        for j in range(TOP_K):
            _row_copy(ys_hbm, 0, buf.at[j], 0, sem).wait()
        return carry

    lax.fori_loop(0, tm, start, 0)
    lax.fori_loop(0, tm, wait, 0)
    slab = slab_ref[...]
    y = slab[:, 2:3] * buf[0] + slab[:, 3:4] * buf[1]
    o_ref[...] = _layer_norm(alpha * x_ref[...] + y, g_ref[...], beta_ref[...])


def _combine(x, slab, dest3, ys, g, beta, alpha):
    M, D = x.shape
    tm = dest3.shape[2] // TOP_K
    row = lambda n: pl.BlockSpec((tm, n), lambda i: (i, 0))
    vec = pl.BlockSpec((1, D), lambda i: (0, 0))
    return pl.pallas_call(
        functools.partial(_combine_kernel, tm=tm, alpha=alpha),
        out_shape=jax.ShapeDtypeStruct((M, D), F32),
        grid=(M // tm,),
        in_specs=[pl.BlockSpec((1, 1, TOP_K * tm), lambda i: (i, 0, 0), memory_space=pltpu.SMEM),
                  row(slab.shape[1]), row(D), vec, vec, pl.BlockSpec(memory_space=pl.ANY)],
        out_specs=row(D),
        scratch_shapes=[pltpu.VMEM((TOP_K, tm, D), F32), pltpu.SemaphoreType.DMA],
        compiler_params=_cparams(("arbitrary",)),
        name="moe_combine",
    )(dest3, slab, x, g, beta, ys)


def _ffn_weights(wg, wu, wd):
    E, D, F = wg.shape
    tf = _tile(F, FF_CHUNK)
    nf = F // tf
    cin = lambda w: w.astype(BF16).reshape(E, D, nf, tf).transpose(0, 2, 1, 3)
    return cin(wg), cin(wu), wd.astype(BF16).reshape(E, nf, tf, D)


def _moe(xp, xs, router_w, router_b, wg, wu, wd, ln_g, ln_b, alpha):
    E = router_w.shape[1]
    D = xp.shape[1]
    pad = ROUTER_PAD - E
    rw = jnp.pad(router_w, ((0, 0), (0, pad)))
    rwh = rw.astype(BF16)
    rwl = (rw - rwh.astype(F32)).astype(BF16)
    rb = jnp.pad(router_b, (0, pad))[None, :]
    slab_p, cnt_p = _router(xp, rwh, rwl, rb, jnp.zeros((1, ROUTER_PAD), F32), E)
    slab_s, cnt = _router(xs, rwh, rwl, rb, cnt_p, E)

    n_assign = TOP_K * (xp.shape[0] + xs.shape[0])
    tm = ROW_TILE
    n_tiles = -(-n_assign // tm) + E
    counts = cnt[0, :E].astype(jnp.int32)
    padded = ((counts + tm - 1) // tm) * tm
    off_end = jnp.cumsum(padded)
    off_start = off_end - padded
    tile_row0 = jnp.arange(n_tiles, dtype=jnp.int32) * tm
    tile_expert = jnp.minimum(jnp.sum(tile_row0[:, None] >= off_end[None, :], axis=1), E - 1)
    te = jnp.concatenate([tile_expert, off_end[-1:] // tm]).astype(jnp.int32)

    def dests(slab):
        e = slab[:, 0:TOP_K].astype(jnp.int32)
        rank = slab[:, 4:4 + TOP_K].astype(jnp.int32)
        d = off_start[e] + rank
        t = _tile(slab.shape[0], ROW_TILE)
        return d.reshape(slab.shape[0] // t, 1, TOP_K * t)

    dest_p, dest_s = dests(slab_p), dests(slab_s)
    xsort = jnp.zeros((n_tiles * tm, D), F32)
    xsort = _dispatch(xp, dest_p, xsort)
    xsort = _dispatch(xs, dest_s, xsort)
    ys = _ffn(xsort, te, wg, wu, wd, tm)
    return (_combine(xp, slab_p, dest_p, ys, ln_g, ln_b, alpha),
            _combine(xs, slab_s, dest_s, ys, ln_g, ln_b, alpha))


def kernel(x_prompt, x_sample, cache_k, cache_v, state_gla, page_table, w_in, w_gate_up, b_gate, sb_bias,
           sb_norm_g, gla_norm_g, w_o, ln1_g, ln1_b, ln2_g, ln2_b, ffn_w_gate, ffn_w_up, ffn_w_down,
           router_w, router_b, moe_w_gate, moe_w_up, moe_w_down):
    B, L, D = x_prompt.shape
    Bd, Ld, _ = x_sample.shape
    depth, n_pool, P, H, Dh = cache_k.shape
    _, _, GH, DK, DV = state_gla.shape
    assert Dh == 64 and DK == 64, "1/sqrt(head dim) must be a power of two to fold into bf16 operands"
    SW, KW, GW = H * Dh, GH * DK, GH * DV
    n_main = 3 * SW + 2 * KW + 2 * GW
    rank = w_gate_up.shape[1]
    alpha = (2 * depth) ** 0.25
    dims = (H, Dh, GH, DK, DV)
    Ls = -(-Ld // GLA_SAMPLE_CHUNK) * GLA_SAMPLE_CHUNK

    xp = x_prompt.reshape(B * L, D)
    xs = x_sample.reshape(Bd * Ld, D)
    cache_kt = cache_k.transpose(0, 1, 3, 4, 2).reshape(depth, n_pool, SW, P)
    cache_vt = cache_v.transpose(0, 1, 3, 4, 2).reshape(depth, n_pool, SW, P)
    head_eye = jnp.eye(H, dtype=BF16)
    row2 = lambda v: v[None, :]
    outs = [[] for _ in range(6)]

    for l in range(depth):
        w_main = jnp.concatenate([w_in[l, :, :SW], w_in[l, :, 3 * SW:n_main]], axis=1).astype(BF16)
        w_kvt = w_in[l, :, SW:3 * SW].T.astype(BF16)
        w_gd = jnp.pad(w_in[l, :, n_main:], ((0, 0), (0, GLA_GATE_PAD - rank))).astype(BF16)
        w_up = jnp.pad(w_gate_up[l], ((0, GLA_GATE_PAD - rank), (0, 0))).astype(BF16)
        bg = row2(b_gate[l])
        wo = w_o[l].astype(BF16)
        g_sb_hm = sb_norm_g[l].reshape(H, 1, Dh)
        g_gla = row2(gla_norm_g[l])

        q, kt, vt, gq, gk, lg, gv, gr = _in_proj(xp, w_main, w_kvt, w_gd, w_up, bg, dims, seq_len=L)
        a = _sb_prompt(q, kt, vt, sb_bias[l], g_sb_hm)
        C = _tile(L, GLA_CHUNK)
        nC = GLA_CHUNKS_PER_STEP
        while (L // C) % nC:
            nC //= 2
        bo, stp = _gla(gq, gk, lg, gv, gr, jnp.zeros((B, GH, DV, DK), F32), g_gla, B, L, C, nC)
        xp = _merge_ln(a, bo, xp, wo[:SW], wo[SW:], row2(ln1_g[l]), row2(ln1_b[l]), alpha)
        outs[0].append(kt.reshape(B, H, Dh, L).transpose(0, 3, 1, 2))
        outs[1].append(vt.reshape(B, H, Dh, L).transpose(0, 3, 1, 2))
        outs[2].append(jnp.swapaxes(stp, -1, -2))

        q, kf, vf, gq, gk, lg, gv, gr = _in_proj(xs, w_main, w_kvt, w_gd, w_up, bg, dims)
        q4 = q.reshape(H, Bd, Ld, Dh).transpose(1, 2, 0, 3)
        qbd = (q4[:, :, :, None, :] * head_eye[None, None, :, :, None]).reshape(Bd, Ld * H, SW)
        pad_new = lambda t: jnp.pad(jnp.swapaxes(t.reshape(Bd, Ld, SW), 1, 2).astype(BF16),
                                    ((0, 0), (0, 0), (0, P - Ld)))
        bias_col = jnp.tile(sb_bias[l], Ld)[:, None]
        a = _sb_decode(l, qbd, pad_new(kf), pad_new(vf), bias_col, row2(sb_norm_g[l]),
                       cache_kt, cache_vt, page_table, Ld, H, Dh)
        a = a.reshape(Bd * Ld, SW).astype(BF16)
        pad_hm = lambda t: jnp.pad(t.reshape(GH, Bd, Ld, DK), ((0, 0), (0, 0), (0, Ls - Ld), (0, 0))
                                   ).reshape(GH, Bd * Ls, DK)
        pad_tm = lambda t: jnp.pad(t.reshape(Bd, Ld, GW), ((0, 0), (0, Ls - Ld), (0, 0))).reshape(Bd * Ls, GW)
        bo, sts = _gla(pad_hm(gq), pad_hm(gk), pad_hm(lg), pad_tm(gv), pad_tm(gr),
                       jnp.swapaxes(state_gla[l], -1, -2), g_gla, Bd, Ls, Ls, 1)
        bo = bo.reshape(Bd, Ls, GW)[:, :Ld].reshape(Bd * Ld, GW)
        xs = _merge_ln(a, bo, xs, wo[:SW], wo[SW:], row2(ln1_g[l]), row2(ln1_b[l]), alpha)
        outs[3].append(kf.reshape(Bd, Ld, H, Dh))
        outs[4].append(vf.reshape(Bd, Ld, H, Dh))
        outs[5].append(jnp.swapaxes(sts, -1, -2))

        i = l // 2
        ln = (row2(ln2_g[l]), row2(ln2_b[l]))
        if l % 2 == 0:
            wg, wu, wd = _ffn_weights(ffn_w_gate[i][None], ffn_w_up[i][None], ffn_w_down[i][None])
            for_x = lambda x: _ffn(x, jnp.array([0] * (x.shape[0] // _tile(x.shape[0], ROW_TILE))
                                                + [x.shape[0] // _tile(x.shape[0], ROW_TILE)], jnp.int32),
                                   wg, wu, wd, _tile(x.shape[0], ROW_TILE), ln=ln, alpha=alpha)
            xp, xs = for_x(xp), for_x(xs)
        else:
            wg, wu, wd = _ffn_weights(moe_w_gate[i], moe_w_up[i], moe_w_down[i])
            xp, xs = _moe(xp, xs, router_w[i], router_b[i], wg, wu, wd, ln[0], ln[1], alpha)

    stack = lambda rows: jnp.stack(rows, axis=0)
    return (xp.reshape(B, L, D), xs.reshape(Bd, Ld, D), stack(outs[0]), stack(outs[1]), stack(outs[2]),
            stack(outs[3]), stack(outs[4]), stack(outs[5]))
```

```python
import functools

import jax
import jax.numpy as jnp
from jax import lax
from jax.experimental import pallas as pl
from jax.experimental.pallas import tpu as pltpu

F32 = jnp.float32
BF16 = jnp.bfloat16

LN_EPS = 1e-5
RMS_EPS = 1e-6
GLA_GATE_NORM = 16.0
GLA_GATE_PAD = 128
ROUTER_PAD = 128
TOP_K = 2
VMEM_LIMIT = 56 * 1024 * 1024

ROW_TILE = 512
SB_TILE = 512
SB_SUB = 128
SB_HEADS_PER_ITER = 2
SB_KEY_CHUNK = 256
GLA_CHUNK = 64
GLA_CHUNKS_PER_STEP = 8
GLA_SAMPLE_CHUNK = 16
FF_CHUNK = 256
DEC_PAGES_PER_STEP = 16


def _tile(n, pref):
    t = min(n, pref)
    while n % t:
        t -= 8
    assert t > 0 and (t % 8 == 0 or t == n)
    return t


def _cparams(sem):
    return pltpu.CompilerParams(dimension_semantics=sem, vmem_limit_bytes=VMEM_LIMIT)


def _softplus(z):
    return jnp.maximum(z, 0.0) + jnp.log(1.0 + jnp.exp(-jnp.abs(z)))


def _split(x):
    hi = x.astype(BF16)
    lo = (x - hi.astype(F32)).astype(BF16)
    return hi, lo


def _dot(a, b):
    return jnp.dot(a, b, preferred_element_type=F32)


def _dot_nt(a, b):
    return lax.dot_general(a, b, (((1,), (1,)), ((), ())), preferred_element_type=F32)


def _dot_tn(a, b):
    return lax.dot_general(a, b, (((0,), (0,)), ((), ())), preferred_element_type=F32)


def _layer_norm(x, g, b):
    mu = jnp.mean(x, axis=-1, keepdims=True)
    xc = x - mu
    var = jnp.mean(xc * xc, axis=-1, keepdims=True)
    return xc * lax.rsqrt(var + LN_EPS) * g + b


def _silu(x):
    return x / (1.0 + jnp.exp(-x))


def _in_proj_kernel(*refs, H, Dh, GH, DK, GW, kv_transposed, n_alias):
    x_ref, w_ref, wkvt_ref, wgd_ref, wup_ref, bg_ref = refs[:6]
    q_ref, k_ref, v_ref, gq_ref, gk_ref, lg_ref, gv_ref, gr_ref = refs[6 + n_alias:]
    xb = x_ref[...].astype(BF16)
    SW = H * Dh
    KW = GH * DK

    def seg(lo, n):
        return _dot(xb, w_ref[:, lo:lo + n])

    if kv_transposed:
        k_ref[...] = _dot_nt(wkvt_ref[0:SW, :], xb)
        v_ref[...] = _dot_nt(wkvt_ref[SW:2 * SW, :], xb)
    else:
        k_ref[...] = _dot_nt(xb, wkvt_ref[0:SW, :])
        v_ref[...] = _dot_nt(xb, wkvt_ref[SW:2 * SW, :])
    pq = seg(0, SW) * (Dh ** -0.5)
    for h in range(H):
        q_ref[h] = pq[:, h * Dh:(h + 1) * Dh].astype(BF16)
    o = SW
    gq = seg(o, KW) * (DK ** -0.5)
    gk = seg(o + KW, KW)
    gv_ref[...] = seg(o + 2 * KW, GW)
    gr_ref[...] = seg(o + 2 * KW + GW, GW)
    gd = _dot(xb, wgd_ref[...]).astype(BF16)
    u = _dot(gd, wup_ref[...]) + bg_ref[...]
    lg = -_softplus(-u) * (1.0 / GLA_GATE_NORM)
    for h in range(GH):
        sl = slice(h * DK, (h + 1) * DK)
        gq_ref[h] = gq[:, sl]
        gk_ref[h] = gk[:, sl]
        lg_ref[h] = lg[:, sl]


def _in_proj(x, w_main, w_kvt, w_gd, w_up, b_gate, dims, seq_len=None, layer=0, depth=1, kv_prev=()):
    H, Dh, GH, DK, DV = dims
    M, D = x.shape
    SW, KW, GW = H * Dh, GH * DK, GH * DV
    tm = _tile(seq_len or M, ROW_TILE)
    row = lambda n: pl.BlockSpec((tm, n), lambda i: (i, 0))
    hm = lambda nh, n: pl.BlockSpec((nh, tm, n), lambda i: (0, i, 0))
    full = lambda a: pl.BlockSpec(a.shape, lambda i: (0,) * a.ndim)
    if seq_len:
        per_seq = seq_len // tm
        n_seq = M // seq_len
        kv_shape = jax.ShapeDtypeStruct((depth * n_seq, SW, seq_len), F32)
        kv_spec = pl.BlockSpec((None, SW, tm), lambda i: (layer * n_seq + i // per_seq, 0, i % per_seq))
    else:
        kv_shape, kv_spec = jax.ShapeDtypeStruct((M, SW), F32), row(SW)
    out_shape = (
        jax.ShapeDtypeStruct((H, M, Dh), BF16), kv_shape, kv_shape,
        jax.ShapeDtypeStruct((GH, M, DK), F32), jax.ShapeDtypeStruct((GH, M, DK), F32),
        jax.ShapeDtypeStruct((GH, M, DK), F32),
        jax.ShapeDtypeStruct((M, GW), F32), jax.ShapeDtypeStruct((M, GW), F32),
    )
    out_specs = (hm(H, Dh), kv_spec, kv_spec, hm(GH, DK), hm(GH, DK), hm(GH, DK), row(GW), row(GW))
    return pl.pallas_call(
        functools.partial(_in_proj_kernel, H=H, Dh=Dh, GH=GH, DK=DK, GW=GW, kv_transposed=bool(seq_len),
                          n_alias=len(kv_prev)),
        out_shape=out_shape,
        grid=(M // tm,),
        in_specs=[row(D), full(w_main), full(w_kvt), full(w_gd), full(w_up), full(b_gate)]
        + [pl.BlockSpec(memory_space=pl.ANY)] * len(kv_prev),
        out_specs=out_specs,
        input_output_aliases={6 + n: 1 + n for n in range(len(kv_prev))},
        compiler_params=_cparams(("parallel",)),
        name="in_proj",
    )(x, w_main, w_kvt, w_gd, w_up, b_gate, *kv_prev)


def _sb_blocks(items, u_ref, Rs, accs):
    zs = [_dot(it["q"], it["kt"]) + it["bias"] for it in items]
    sps = [_softplus(z) if it["mask"] is None else jnp.where(it["mask"], _softplus(z), 0.0)
           for it, z in zip(items, zs)]
    parts = [_split(sp) for sp in sps]
    S = [None] * len(items)
    for w in sorted({z.shape[1] for z in zs}):
        idx = [n for n, z in enumerate(zs) if z.shape[1] == w]
        stack = jnp.concatenate([jnp.concatenate(parts[n], axis=1) for n in idx], axis=0)
        uu = jnp.concatenate([u_ref[0:w, 0:w]] * 2, axis=0)
        out = _dot(stack, uu)
        o = 0
        for n in idx:
            nr = zs[n].shape[0]
            S[n] = out[o:o + nr]
            o += nr
    Rs, accs = list(Rs), list(accs)
    for n, it in enumerate(items):
        r = it["row"]
        logw = zs[n] - S[n] - Rs[r]
        if it["mask"] is not None:
            logw = jnp.where(it["mask"], logw, -1e30)
        accs[r] = accs[r] + _dot_nt(jnp.exp(logw).astype(BF16), it["vt"])
        Rs[r] = Rs[r] + S[n][:, 0:1]
    return Rs, accs


def _sb_prompt_kernel(qi_ref, kj_ref, bias_ref, q_ref, kt_ref, vt_ref, g_ref, u_ref, o_ref, r_scr, acc_scr,
                      *, H, Dh, T, SUB, KC):
    p = pl.program_id(1)
    i, j = qi_ref[p], kj_ref[p]
    nsub = T // SUB
    HG = SB_HEADS_PER_ITER

    def run(diag):
        def head_group(hg, carry):
            rs = [slice(r * SUB, (r + 1) * SUB) for r in range(nsub)]
            items, Rs, accs = [], [], []
            for hh in range(HG):
                h = hg * HG + hh
                bias = bias_ref[h]
                hs = pl.ds(pl.multiple_of(h * Dh, Dh), Dh)
                for r in range(nsub):
                    if diag:
                        Rs.append(jnp.zeros((SUB, 1), F32))
                        accs.append(jnp.zeros((SUB, Dh), F32))
                    else:
                        Rs.append(r_scr[h, rs[r], :])
                        accs.append(acc_scr[h, rs[r], :])
                    q = q_ref[h, rs[r], :]
                    c1 = (r + 1) * SUB if diag else T
                    while c1 > 0:
                        c0 = ((c1 - 1) // KC) * KC
                        mask = None
                        if diag and c1 > r * SUB:
                            rows = lax.broadcasted_iota(jnp.int32, (SUB, c1 - c0), 0) + r * SUB
                            cols = lax.broadcasted_iota(jnp.int32, (SUB, c1 - c0), 1) + c0
                            mask = cols < rows
                        items.append(dict(row=hh * nsub + r, q=q, kt=kt_ref[hs, c0:c1].astype(BF16),
                                          vt=vt_ref[hs, c0:c1].astype(BF16), bias=bias, mask=mask))
                        c1 = c0
            Rs, accs = _sb_blocks(items, u_ref, Rs, accs)
            for hh in range(HG):
                h = hg * HG + hh
                for r in range(nsub):
                    r_scr[h, rs[r], :] = Rs[hh * nsub + r]
                    acc_scr[h, rs[r], :] = accs[hh * nsub + r]
            return carry

        lax.fori_loop(0, H // HG, head_group, 0)

    @pl.when(i == j)
    def _():
        run(True)

    @pl.when(i != j)
    def _():
        run(False)

    @pl.when(j == 0)
    def _():
        def normed(h):
            o = acc_scr[h]
            ms = jnp.mean(o * o, axis=-1, keepdims=True)
            return o * lax.rsqrt(ms + RMS_EPS) * g_ref[h]

        for hp in range(H // 2):
            o_ref[:, 2 * hp * Dh:(2 * hp + 2) * Dh] = jnp.concatenate(
                [normed(2 * hp), normed(2 * hp + 1)], axis=-1).astype(BF16)


def _sb_prompt(q_hm, kt, vt, bias, g_hm, seq0=0):
    H, M, Dh = q_hm.shape
    _, SW, L = kt.shape
    B = M // L
    T = _tile(L, SB_TILE)
    SUB = _tile(T, SB_SUB)
    KC = _tile(T, SB_KEY_CHUNK)
    nq = L // T
    assert H % SB_HEADS_PER_ITER == 0
    pairs = [(i, j) for i in range(nq) for j in range(i, -1, -1)]
    qi = jnp.array([p[0] for p in pairs], jnp.int32)
    kj = jnp.array([p[1] for p in pairs], jnp.int32)
    u = (jnp.arange(KC)[:, None] >= jnp.arange(KC)[None, :]).astype(BF16)
    kv_spec = pl.BlockSpec((None, SW, T), lambda b, p, qi, kj: (seq0 + b, 0, kj[p]))
    grid_spec = pltpu.PrefetchScalarGridSpec(
        num_scalar_prefetch=2,
        grid=(B, len(pairs)),
        in_specs=[
            pl.BlockSpec(memory_space=pltpu.SMEM),
            pl.BlockSpec((H, T, Dh), lambda b, p, qi, kj: (0, b * nq + qi[p], 0)),
            kv_spec, kv_spec,
            pl.BlockSpec((H, 1, Dh), lambda b, p, qi, kj: (0, 0, 0)),
            pl.BlockSpec((KC, KC), lambda b, p, qi, kj: (0, 0)),
        ],
        out_specs=pl.BlockSpec((T, SW), lambda b, p, qi, kj: (b * nq + qi[p], 0)),
        scratch_shapes=[pltpu.VMEM((H, T, 1), F32), pltpu.VMEM((H, T, Dh), F32)],
    )
    return pl.pallas_call(
        functools.partial(_sb_prompt_kernel, H=H, Dh=Dh, T=T, SUB=SUB, KC=KC),
        out_shape=jax.ShapeDtypeStruct((M, SW), BF16),
        grid_spec=grid_spec,
        compiler_params=_cparams(("parallel", "arbitrary")),
        name="sb_prompt",
    )(qi, kj, bias, q_hm, kt, vt, g_hm, u)


def _sb_decode_kernel(pt_ref, qbd_ref, kn_ref, vn_ref, bias_ref, g_ref, own_ref, u_ref, ck_hbm, cv_hbm,
                      o_ref, kbuf, vbuf, sem, acc_scr, r_scr, *, layer, G, Ld, H, Dh):
    b, s = pl.program_id(0), pl.program_id(1)
    nb, ns = pl.num_programs(0), pl.num_programs(1)
    NP = ns * G
    HQ = Ld * H
    P = u_ref.shape[0]
    t = b * ns + s
    slot = lax.rem(t, 2)

    def page_copies(bb, ss, sl):
        cps = []
        for i in range(G):
            page = pt_ref[bb, NP - 1 - (ss * G + i)]
            cps.append(pltpu.make_async_copy(ck_hbm.at[layer, page], kbuf.at[sl, i], sem.at[sl]))
            cps.append(pltpu.make_async_copy(cv_hbm.at[layer, page], vbuf.at[sl, i], sem.at[sl]))
        return cps

    @pl.when(t == 0)
    def _():
        for cp in page_copies(b, s, slot):
            cp.start()

    @pl.when(t + 1 < nb * ns)
    def _():
        wrap = s + 1 == ns
        for cp in page_copies(jnp.where(wrap, b + 1, b), jnp.where(wrap, 0, s + 1), 1 - slot):
            cp.start()

    for cp in page_copies(b, s, slot):
        cp.wait()

    q = qbd_ref[...]
    bias = bias_ref[...]

    @pl.when(s == 0)
    def _():
        rows = lax.broadcasted_iota(jnp.int32, (HQ, P), 0)
        cols = lax.broadcasted_iota(jnp.int32, (HQ, P), 1)
        mask = rows >= cols * H + H
        Rs, accs = _sb_blocks([dict(row=0, q=q, kt=kn_ref[...], vt=vn_ref[...], bias=bias, mask=mask)], u_ref,
                              [jnp.zeros((HQ, 1), F32)], [jnp.zeros((HQ, H * Dh), F32)])
        r_scr[...] = Rs[0]
        acc_scr[...] = accs[0]

    items = [dict(row=0, q=q, kt=kbuf[slot, i].astype(BF16), vt=vbuf[slot, i].astype(BF16), bias=bias, mask=None)
             for i in range(G)]
    Rs, accs = _sb_blocks(items, u_ref, [r_scr[...]], [acc_scr[...]])
    r_scr[...] = Rs[0]
    acc_scr[...] = accs[0]

    @pl.when(s == pl.num_programs(1) - 1)
    def _():
        o = acc_scr[...] * own_ref[...]
        ms = jnp.sum(o * o, axis=-1, keepdims=True) * (1.0 / Dh)
        o = o * lax.rsqrt(ms + RMS_EPS)
        g = g_ref[...]
        for t in range(Ld):
            o_ref[t:t + 1, :] = jnp.sum(o[t * H:(t + 1) * H, :], axis=0, keepdims=True) * g


def _sb_decode(layer, qbd, kt_new, vt_new, bias_col, g, cache_kt, cache_vt, page_table, Ld, H, Dh):
    Bd, HQ, SW = qbd.shape
    NP = page_table.shape[1]
    P = cache_kt.shape[3]
    G = DEC_PAGES_PER_STEP
    while NP % G:
        G //= 2
    u = (jnp.arange(P)[:, None] >= jnp.arange(P)[None, :]).astype(BF16)
    own = (jnp.arange(HQ)[:, None] % H == jnp.arange(SW)[None, :] // Dh).astype(F32)

    per_b = lambda r, c: pl.BlockSpec((None, r, c), lambda b, s, pt: (b, 0, 0))
    const = lambda a: pl.BlockSpec(a.shape, lambda b, s, pt: (0,) * a.ndim)
    hbm = pl.BlockSpec(memory_space=pl.ANY)
    grid_spec = pltpu.PrefetchScalarGridSpec(
        num_scalar_prefetch=1,
        grid=(Bd, NP // G),
        in_specs=[per_b(HQ, SW), per_b(SW, P), per_b(SW, P), const(bias_col), const(g), const(own), const(u),
                  hbm, hbm],
        out_specs=per_b(Ld, SW),
        scratch_shapes=[pltpu.VMEM((2, G, SW, P), F32), pltpu.VMEM((2, G, SW, P), F32),
                        pltpu.SemaphoreType.DMA((2,)),
                        pltpu.VMEM((HQ, SW), F32), pltpu.VMEM((HQ, 1), F32)],
    )
    return pl.pallas_call(
        functools.partial(_sb_decode_kernel, layer=layer, G=G, Ld=Ld, H=H, Dh=Dh),
        out_shape=jax.ShapeDtypeStruct((Bd, Ld, SW), F32),
        grid_spec=grid_spec,
        compiler_params=_cparams(("arbitrary", "arbitrary")),
        name="sb_decode",
    )(page_table, qbd, kt_new, vt_new, bias_col, g, own, u, cache_kt, cache_vt)


def _gla_kernel(q_ref, k_ref, lg_ref, v_ref, r_ref, s0_ref, g_ref, ltri2_ref, cmask_ref, o_ref, sout_ref, st_scr,
                *, C, nC):
    c = pl.program_id(2)

    @pl.when(c == 0)
    def _():
        st_scr[...] = s0_ref[...]

    cs = [slice(ci * C, (ci + 1) * C) for ci in range(nC)]
    q, k, lg = q_ref[...], k_ref[...], lg_ref[...]
    vb = v_ref[...].astype(BF16)
    hi, lo = _split(lg)
    ltri2 = ltri2_ref[...]
    causal = cmask_ref[...] > 0.5
    b = jnp.concatenate([_dot(ltri2, jnp.concatenate([hi[sl], lo[sl]], axis=0)) for sl in cs], axis=0)
    b_last = [b[(ci + 1) * C - 1:(ci + 1) * C, :] for ci in range(nC)]
    b_end = jnp.concatenate([jnp.broadcast_to(bl, (C, bl.shape[1])) for bl in b_last], axis=0)
    a = (q * jnp.exp(b)).astype(BF16)
    kd = (k * jnp.exp(-b)).astype(BF16)
    k2 = (k * jnp.exp(b_end - b)).astype(BF16)
    scs = [jnp.where(causal, _dot_nt(a[sl], kd[sl]), 0.0).astype(BF16) for sl in cs]
    o_intra = jnp.concatenate([_dot(sc, vb[sl]) for sc, sl in zip(scs, cs)], axis=0)
    upd = [_dot_tn(vb[sl], k2[sl]) for sl in cs]
    st = st_scr[...]
    o_inter = []
    for ci in range(nC):
        o_inter.append(_dot_nt(a[cs[ci]], st.astype(BF16)))
        st = st * jnp.exp(b_last[ci]) + upd[ci]
    st_scr[...] = st
    o = o_intra + jnp.concatenate(o_inter, axis=0)
    ms = jnp.mean(o * o, axis=-1, keepdims=True)
    o_ref[...] = (o * lax.rsqrt(ms + RMS_EPS) * g_ref[...] * _silu(r_ref[...])).astype(BF16)

    @pl.when(c == pl.num_programs(2) - 1)
    def _():
        sout_ref[...] = st_scr[...]


def _gla(gq, gk, lg, gv, gr, s0t, g, B, L, C, nC):
    GH, M, DK = gq.shape
    DV = gv.shape[1] // GH
    blk = C * nC
    nblk = L // blk
    t = jnp.arange(C)
    causal = t[:, None] >= t[None, :]
    ltri2 = jnp.concatenate([causal, causal], axis=1).astype(BF16)
    cmask = causal.astype(F32)
    hm = pl.BlockSpec((None, blk, DK), lambda b, h, c: (h, b * nblk + c, 0))
    tokm = pl.BlockSpec((blk, DV), lambda b, h, c: (b * nblk + c, h))
    state = pl.BlockSpec((None, None, DV, DK), lambda b, h, c: (b, h, 0, 0))
    const = lambda a: pl.BlockSpec(a.shape, lambda b, h, c: (0,) * a.ndim)
    return pl.pallas_call(
        functools.partial(_gla_kernel, C=C, nC=nC),
        out_shape=(jax.ShapeDtypeStruct((M, GH * DV), BF16), jax.ShapeDtypeStruct((B, GH, DV, DK), F32)),
        grid=(B, GH, nblk),
        in_specs=[hm, hm, hm, tokm, tokm, state,
                  pl.BlockSpec((1, DV), lambda b, h, c: (0, h)), const(ltri2), const(cmask)],
        out_specs=(tokm, state),
        scratch_shapes=[pltpu.VMEM((DV, DK), F32)],
        compiler_params=_cparams(("parallel", "parallel", "arbitrary")),
        name="gla",
    )(gq, gk, lg, gv, gr, s0t, g, ltri2, cmask)


def _merge_ln_kernel(a_ref, b_ref, x_ref, wa_ref, wb_ref, g_ref, beta_ref, o_ref, *, alpha):
    y = _dot(a_ref[...], wa_ref[...]) + _dot(b_ref[...], wb_ref[...])
    o_ref[...] = _layer_norm(alpha * x_ref[...] + y, g_ref[...], beta_ref[...])


def _merge_ln(a, b, x, wa, wb, g, beta, alpha):
    M, D = x.shape
    tm = _tile(M, ROW_TILE)
    row = lambda n: pl.BlockSpec((tm, n), lambda i: (i, 0))
    full = lambda t: pl.BlockSpec(t.shape, lambda i: (0,) * t.ndim)
    return pl.pallas_call(
        functools.partial(_merge_ln_kernel, alpha=alpha),
        out_shape=jax.ShapeDtypeStruct((M, D), F32),
        grid=(M // tm,),
        in_specs=[row(a.shape[1]), row(b.shape[1]), row(D), full(wa), full(wb), full(g), full(beta)],
        out_specs=row(D),
        compiler_params=_cparams(("parallel",)),
        name="merge_ln",
    )(a, b, x, wa, wb, g, beta)


def _ffn_kernel(te_ref, x_ref, wg_ref, wu_ref, wd_ref, *rest, nf, n_tiles, alpha, fuse_ln):
    if fuse_ln:
        g_ref, beta_ref, o_ref, xb_scr, acc_scr = rest
    else:
        o_ref, xb_scr, acc_scr = rest
    i = pl.program_id(0)

    @pl.when(i < te_ref[n_tiles])
    def _():
        xb_scr[...] = x_ref[...].astype(BF16)
        acc_scr[...] = jnp.zeros_like(acc_scr)

        def body(f, carry):
            xb = xb_scr[...]
            hg = _dot(xb, wg_ref[f])
            hu = _dot(xb, wu_ref[f])
            acc_scr[...] += _dot((_silu(hg) * hu).astype(BF16), wd_ref[f])
            return carry

        lax.fori_loop(0, nf, body, 0)
        if fuse_ln:
            o_ref[...] = _layer_norm(alpha * x_ref[...] + acc_scr[...], g_ref[...], beta_ref[...])
        else:
            o_ref[...] = acc_scr[...]

    @pl.when(i >= te_ref[n_tiles])
    def _():
        o_ref[...] = jnp.zeros_like(o_ref)


def _ffn(x, te, wg, wu, wd, tm, ln=None, alpha=1.0):
    R, D = x.shape
    _, nf, _, tf = wg.shape
    n_tiles = R // tm
    row = pl.BlockSpec((tm, D), lambda i, te: (i, 0))
    w_in = pl.BlockSpec((None, nf, D, tf), lambda i, te: (te[i], 0, 0, 0))
    w_out = pl.BlockSpec((None, nf, tf, D), lambda i, te: (te[i], 0, 0, 0))
    vec = pl.BlockSpec((1, D), lambda i, te: (0, 0))
    in_specs = [row, w_in, w_in, w_out] + ([vec, vec] if ln else [])
    grid_spec = pltpu.PrefetchScalarGridSpec(
        num_scalar_prefetch=1, grid=(n_tiles,), in_specs=in_specs, out_specs=row,
        scratch_shapes=[pltpu.VMEM((tm, D), BF16), pltpu.VMEM((tm, D), F32)])
    return pl.pallas_call(
        functools.partial(_ffn_kernel, nf=nf, n_tiles=n_tiles, alpha=alpha, fuse_ln=bool(ln)),
        out_shape=jax.ShapeDtypeStruct((R, D), F32),
        grid_spec=grid_spec,
        compiler_params=_cparams(("arbitrary",)),
        name="ffn_ln" if ln else "ffn_experts",
    )(te, x, wg, wu, wd, *(ln or ()))


def _router_kernel(x_ref, wh_ref, wl_ref, b_ref, c0_ref, lstrict_ref, slab_ref, cnt_ref, carry_scr, *, E):
    i = pl.program_id(0)

    @pl.when(i == 0)
    def _():
        carry_scr[...] = c0_ref[...]

    xh, xl = _split(x_ref[...])
    wh, wl = wh_ref[...], wl_ref[...]
    logits = _dot(xh, wh) + (_dot(xl, wh) + _dot(xh, wl)) + b_ref[...]
    tm, W = logits.shape
    lane = lax.broadcasted_iota(jnp.int32, (tm, W), 1).astype(F32)
    neg = -jnp.inf
    lg1 = jnp.where(lane < E, logits, neg)
    m1 = jnp.max(lg1, axis=-1, keepdims=True)
    i1 = jnp.min(jnp.where(lg1 == m1, lane, float(W)), axis=-1, keepdims=True)
    lg2 = jnp.where(lane == i1, neg, lg1)
    m2 = jnp.max(lg2, axis=-1, keepdims=True)
    i2 = jnp.min(jnp.where(lg2 == m2, lane, float(W)), axis=-1, keepdims=True)
    e2 = jnp.exp(m2 - m1)
    g1 = 1.0 / (1.0 + e2)
    g2 = e2 / (1.0 + e2)
    oh = jnp.where((lane == i1) | (lane == i2), 1.0, 0.0)
    before = _dot(lstrict_ref[...], oh.astype(BF16)) + carry_scr[...]
    r1 = jnp.sum(jnp.where(lane == i1, before, 0.0), axis=-1, keepdims=True)
    r2 = jnp.sum(jnp.where(lane == i2, before, 0.0), axis=-1, keepdims=True)
    carry_scr[...] += jnp.sum(oh, axis=0, keepdims=True)
    slab = jnp.zeros((tm, W), F32)
    for col, val in enumerate((i1, i2, g1, g2, r1, r2)):
        slab = jnp.where(lane == col, val, slab)
    slab_ref[...] = slab
    cnt_ref[...] = carry_scr[...]


def _router(x, wh, wl, b, c0, E):
    M, D = x.shape
    tm = _tile(M, ROW_TILE)
    W = wh.shape[1]
    lstrict = (jnp.arange(tm)[:, None] > jnp.arange(tm)[None, :]).astype(BF16)
    full = lambda t: pl.BlockSpec(t.shape, lambda i: (0,) * t.ndim)
    return pl.pallas_call(
        functools.partial(_router_kernel, E=E),
        out_shape=(jax.ShapeDtypeStruct((M, W), F32), jax.ShapeDtypeStruct((1, W), F32)),
        grid=(M // tm,),
        in_specs=[pl.BlockSpec((tm, D), lambda i: (i, 0)), full(wh), full(wl), full(b), full(c0), full(lstrict)],
        out_specs=(pl.BlockSpec((tm, W), lambda i: (i, 0)), pl.BlockSpec((1, W), lambda i: (0, 0))),
        scratch_shapes=[pltpu.VMEM((1, W), F32)],
        compiler_params=_cparams(("arbitrary",)),
        name="router",
    )(x, wh, wl, b, c0, lstrict)


def _row_copy(src, s, dst, d, sem):
    return pltpu.make_async_copy(src.at[pl.ds(s, 1)], dst.at[pl.ds(d, 1)], sem)


def _dispatch_kernel(dest_ref, x_ref, xs_in, xs_out, sem, *, tm):
    del xs_in

    def start(r, carry):
        for j in range(TOP_K):
            _row_copy(x_ref, r, xs_out, dest_ref[0, 0, TOP_K * r + j], sem).start()
        return carry

    def wait(r, carry):
        for j in range(TOP_K):
            _row_copy(x_ref, 0, xs_out, 0, sem).wait()
        return carry

    lax.fori_loop(0, tm, start, 0)
    lax.fori_loop(0, tm, wait, 0)


def _dispatch(x, dest3, xs):
    M, D = x.shape
    tm = dest3.shape[2] // TOP_K
    return pl.pallas_call(
        functools.partial(_dispatch_kernel, tm=tm),
        out_shape=jax.ShapeDtypeStruct(xs.shape, xs.dtype),
        grid=(M // tm,),
        in_specs=[pl.BlockSpec((1, 1, TOP_K * tm), lambda i: (i, 0, 0), memory_space=pltpu.SMEM),
                  pl.BlockSpec((tm, D), lambda i: (i, 0)), pl.BlockSpec(memory_space=pl.ANY)],
        out_specs=pl.BlockSpec(memory_space=pl.ANY),
        scratch_shapes=[pltpu.SemaphoreType.DMA],
        input_output_aliases={2: 0},
        compiler_params=_cparams(("arbitrary",)),
        name="moe_dispatch",
    )(dest3, x, xs)


def _combine_kernel(dest_ref, slab_ref, x_ref, g_ref, beta_ref, ys_hbm, o_ref, buf, sem, *, tm, alpha):
    def start(r, carry):
        for j in range(TOP_K):
            _row_copy(ys_hbm, dest_ref[0, 0, TOP_K * r + j], buf.at[j], r, sem).start()
        return carry

    def wait(r, carry):
        for j in range(TOP_K):
            _row_copy(ys_hbm, 0, buf.at[j], 0, sem).wait()
        return carry

    lax.fori_loop(0, tm, start, 0)
    lax.fori_loop(0, tm, wait, 0)
    slab = slab_ref[...]
    y = slab[:, 2:3] * buf[0] + slab[:, 3:4] * buf[1]
    o_ref[...] = _layer_norm(alpha * x_ref[...] + y, g_ref[...], beta_ref[...])


def _combine(x, slab, dest3, ys, g, beta, alpha):
    M, D = x.shape
    tm = dest3.shape[2] // TOP_K
    row = lambda n: pl.BlockSpec((tm, n), lambda i: (i, 0))
    vec = pl.BlockSpec((1, D), lambda i: (0, 0))
    return pl.pallas_call(
        functools.partial(_combine_kernel, tm=tm, alpha=alpha),
        out_shape=jax.ShapeDtypeStruct((M, D), F32),
        grid=(M // tm,),
        in_specs=[pl.BlockSpec((1, 1, TOP_K * tm), lambda i: (i, 0, 0), memory_space=pltpu.SMEM),
                  row(slab.shape[1]), row(D), vec, vec, pl.BlockSpec(memory_space=pl.ANY)],
        out_specs=row(D),
        scratch_shapes=[pltpu.VMEM((TOP_K, tm, D), F32), pltpu.SemaphoreType.DMA],
        compiler_params=_cparams(("arbitrary",)),
        name="moe_combine",
    )(dest3, slab, x, g, beta, ys)


def _ffn_weights(wg, wu, wd):
    E, D, F = wg.shape
    tf = _tile(F, FF_CHUNK)
    nf = F // tf
    cin = lambda w: w.astype(BF16).reshape(E, D, nf, tf).transpose(0, 2, 1, 3)
    return cin(wg), cin(wu), wd.astype(BF16).reshape(E, nf, tf, D)


def _moe(xp, xs, router_w, router_b, wg, wu, wd, ln_g, ln_b, alpha):
    E = router_w.shape[1]
    D = xp.shape[1]
    pad = ROUTER_PAD - E
    rw = jnp.pad(router_w, ((0, 0), (0, pad)))
    rwh = rw.astype(BF16)
    rwl = (rw - rwh.astype(F32)).astype(BF16)
    rb = jnp.pad(router_b, (0, pad))[None, :]
    slab_p, cnt_p = _router(xp, rwh, rwl, rb, jnp.zeros((1, ROUTER_PAD), F32), E)
    slab_s, cnt = _router(xs, rwh, rwl, rb, cnt_p, E)

    n_assign = TOP_K * (xp.shape[0] + xs.shape[0])
    tm = ROW_TILE
    n_tiles = -(-n_assign // tm) + E
    counts = cnt[0, :E].astype(jnp.int32)
    padded = ((counts + tm - 1) // tm) * tm
    off_end = jnp.cumsum(padded)
    off_start = off_end - padded
    tile_row0 = jnp.arange(n_tiles, dtype=jnp.int32) * tm
    tile_expert = jnp.minimum(jnp.sum(tile_row0[:, None] >= off_end[None, :], axis=1), E - 1)
    te = jnp.concatenate([tile_expert, off_end[-1:] // tm]).astype(jnp.int32)

    def dests(slab):
        e = slab[:, 0:TOP_K].astype(jnp.int32)
        rank = slab[:, 4:4 + TOP_K].astype(jnp.int32)
        d = off_start[e] + rank
        t = _tile(slab.shape[0], ROW_TILE)
        return d.reshape(slab.shape[0] // t, 1, TOP_K * t)

    dest_p, dest_s = dests(slab_p), dests(slab_s)
    xsort = jnp.zeros((n_tiles * tm, D), F32)
    xsort = _dispatch(xp, dest_p, xsort)
    xsort = _dispatch(xs, dest_s, xsort)
    ys = _ffn(xsort, te, wg, wu, wd, tm)
    return (_combine(xp, slab_p, dest_p, ys, ln_g, ln_b, alpha),
            _combine(xs, slab_s, dest_s, ys, ln_g, ln_b, alpha))


def kernel(x_prompt, x_sample, cache_k, cache_v, state_gla, page_table, w_in, w_gate_up, b_gate, sb_bias,
           sb_norm_g, gla_norm_g, w_o, ln1_g, ln1_b, ln2_g, ln2_b, ffn_w_gate, ffn_w_up, ffn_w_down,
           router_w, router_b, moe_w_gate, moe_w_up, moe_w_down):
    B, L, D = x_prompt.shape
    Bd, Ld, _ = x_sample.shape
    depth, n_pool, P, H, Dh = cache_k.shape
    _, _, GH, DK, DV = state_gla.shape
    assert Dh == 64 and DK == 64, "1/sqrt(head dim) must be a power of two to fold into bf16 operands"
    SW, KW, GW = H * Dh, GH * DK, GH * DV
    n_main = 3 * SW + 2 * KW + 2 * GW
    rank = w_gate_up.shape[1]
    alpha = (2 * depth) ** 0.25
    dims = (H, Dh, GH, DK, DV)
    Ls = -(-Ld // GLA_SAMPLE_CHUNK) * GLA_SAMPLE_CHUNK

    xp = x_prompt.reshape(B * L, D)
    xs = x_sample.reshape(Bd * Ld, D)
    cache_kt = cache_k.transpose(0, 1, 3, 4, 2).reshape(depth, n_pool, SW, P)
    cache_vt = cache_v.transpose(0, 1, 3, 4, 2).reshape(depth, n_pool, SW, P)
    head_eye = jnp.eye(H, dtype=BF16)
    row2 = lambda v: v[None, :]
    outs = [[] for _ in range(6)]
    kv_prompt = (jnp.zeros((depth * B, SW, L), F32), jnp.zeros((depth * B, SW, L), F32))

    for l in range(depth):
        w_main = jnp.concatenate([w_in[l, :, :SW], w_in[l, :, 3 * SW:n_main]], axis=1).astype(BF16)
        w_kvt = w_in[l, :, SW:3 * SW].T.astype(BF16)
        w_gd = jnp.pad(w_in[l, :, n_main:], ((0, 0), (0, GLA_GATE_PAD - rank))).astype(BF16)
        w_up = jnp.pad(w_gate_up[l], ((0, GLA_GATE_PAD - rank), (0, 0))).astype(BF16)
        bg = row2(b_gate[l])
        wo = w_o[l].astype(BF16)
        g_sb_hm = sb_norm_g[l].reshape(H, 1, Dh)
        g_gla = row2(gla_norm_g[l])

        q, kt, vt, gq, gk, lg, gv, gr = _in_proj(xp, w_main, w_kvt, w_gd, w_up, bg, dims, seq_len=L,
                                                 layer=l, depth=depth, kv_prev=kv_prompt)
        kv_prompt = (kt, vt)
        a = _sb_prompt(q, kt, vt, sb_bias[l], g_sb_hm, seq0=l * B)
        C = _tile(L, GLA_CHUNK)
        nC = GLA_CHUNKS_PER_STEP
        while (L // C) % nC:
            nC //= 2
        bo, stp = _gla(gq, gk, lg, gv, gr, jnp.zeros((B, GH, DV, DK), F32), g_gla, B, L, C, nC)
        xp = _merge_ln(a, bo, xp, wo[:SW], wo[SW:], row2(ln1_g[l]), row2(ln1_b[l]), alpha)
        outs[2].append(jnp.swapaxes(stp, -1, -2))

        q, kf, vf, gq, gk, lg, gv, gr = _in_proj(xs, w_main, w_kvt, w_gd, w_up, bg, dims)
        q4 = q.reshape(H, Bd, Ld, Dh).transpose(1, 2, 0, 3)
        qbd = (q4[:, :, :, None, :] * head_eye[None, None, :, :, None]).reshape(Bd, Ld * H, SW)
        pad_new = lambda t: jnp.pad(jnp.swapaxes(t.reshape(Bd, Ld, SW), 1, 2).astype(BF16),
                                    ((0, 0), (0, 0), (0, P - Ld)))
        bias_col = jnp.tile(sb_bias[l], Ld)[:, None]
        a = _sb_decode(l, qbd, pad_new(kf), pad_new(vf), bias_col, row2(sb_norm_g[l]),
                       cache_kt, cache_vt, page_table, Ld, H, Dh)
        a = a.reshape(Bd * Ld, SW).astype(BF16)
        pad_hm = lambda t: jnp.pad(t.reshape(GH, Bd, Ld, DK), ((0, 0), (0, 0), (0, Ls - Ld), (0, 0))
                                   ).reshape(GH, Bd * Ls, DK)
        pad_tm = lambda t: jnp.pad(t.reshape(Bd, Ld, GW), ((0, 0), (0, Ls - Ld), (0, 0))).reshape(Bd * Ls, GW)
        bo, sts = _gla(pad_hm(gq), pad_hm(gk), pad_hm(lg), pad_tm(gv), pad_tm(gr),
                       jnp.swapaxes(state_gla[l], -1, -2), g_gla, Bd, Ls, Ls, 1)
        bo = bo.reshape(Bd, Ls, GW)[:, :Ld].reshape(Bd * Ld, GW)
        xs = _merge_ln(a, bo, xs, wo[:SW], wo[SW:], row2(ln1_g[l]), row2(ln1_b[l]), alpha)
        outs[3].append(kf.reshape(Bd, Ld, H, Dh))
        outs[4].append(vf.reshape(Bd, Ld, H, Dh))
        outs[5].append(jnp.swapaxes(sts, -1, -2))

        i = l // 2
        ln = (row2(ln2_g[l]), row2(ln2_b[l]))
        if l % 2 == 0:
            wg, wu, wd = _ffn_weights(ffn_w_gate[i][None], ffn_w_up[i][None], ffn_w_down[i][None])
            for_x = lambda x: _ffn(x, jnp.array([0] * (x.shape[0] // _tile(x.shape[0], ROW_TILE))
                                                + [x.shape[0] // _tile(x.shape[0], ROW_TILE)], jnp.int32),
                                   wg, wu, wd, _tile(x.shape[0], ROW_TILE), ln=ln, alpha=alpha)
            xp, xs = for_x(xp), for_x(xs)
        else:
            wg, wu, wd = _ffn_weights(moe_w_gate[i], moe_w_up[i], moe_w_down[i])
            xp, xs = _moe(xp, xs, router_w[i], router_b[i], wg, wu, wd, ln[0], ln[1], alpha)

    stack = lambda rows: jnp.stack(rows, axis=0)
    kv_out = lambda t: t.reshape(depth, B, H, Dh, L).transpose(0, 1, 4, 2, 3)
    return (xp.reshape(B, L, D), xs.reshape(Bd, Ld, D), kv_out(kv_prompt[0]), kv_out(kv_prompt[1]), stack(outs[2]),
            stack(outs[3]), stack(outs[4]), stack(outs[5]))
```

```python
import functools

import jax
import jax.numpy as jnp
from jax import lax
from jax.experimental import pallas as pl
from jax.experimental.pallas import tpu as pltpu

F32 = jnp.float32
BF16 = jnp.bfloat16

LN_EPS = 1e-5
RMS_EPS = 1e-6
GLA_GATE_NORM = 16.0
GLA_GATE_PAD = 128
ROUTER_PAD = 128
TOP_K = 2
VMEM_LIMIT = 56 * 1024 * 1024

ROW_TILE = 512
SB_TILE = 512
SB_SUB = 128
SB_HEADS_PER_ITER = 4
SB_KEY_CHUNK = 256
GLA_CHUNK = 64
GLA_CHUNKS_PER_STEP = 8
GLA_SAMPLE_CHUNK = 16
GLA_HEADS_PER_STEP = 2
FF_CHUNK = 256
DEC_PAGES_PER_STEP = 16
ROW_DMA_UNROLL = 8


def _tile(n, pref):
    t = min(n, pref)
    while n % t:
        t -= 8
    assert t > 0 and (t % 8 == 0 or t == n)
    return t


def _cparams(sem):
    return pltpu.CompilerParams(dimension_semantics=sem, vmem_limit_bytes=VMEM_LIMIT)


def _softplus(z):
    return jnp.maximum(z, 0.0) + jnp.log(1.0 + jnp.exp(-jnp.abs(z)))


def _split(x):
    hi = x.astype(BF16)
    lo = (x - hi.astype(F32)).astype(BF16)
    return hi, lo


def _dot(a, b):
    return jnp.dot(a, b, preferred_element_type=F32)


def _dot_nt(a, b):
    return lax.dot_general(a, b, (((1,), (1,)), ((), ())), preferred_element_type=F32)


def _dot_tn(a, b):
    return lax.dot_general(a, b, (((0,), (0,)), ((), ())), preferred_element_type=F32)


def _layer_norm(x, g, b):
    mu = jnp.mean(x, axis=-1, keepdims=True)
    xc = x - mu
    var = jnp.mean(xc * xc, axis=-1, keepdims=True)
    return xc * lax.rsqrt(var + LN_EPS) * g + b


def _silu(x):
    return x / (1.0 + jnp.exp(-x))


def _in_proj_kernel(*refs, H, Dh, GH, DK, GW, kv_transposed, n_alias):
    x_ref, w_ref, wkvt_ref, wgd_ref, wup_ref, bg_ref = refs[:6]
    q_ref, k_ref, v_ref, gq_ref, gk_ref, lg_ref, gv_ref, gr_ref = refs[6 + n_alias:]
    xb = x_ref[...].astype(BF16)
    SW = H * Dh
    KW = GH * DK

    def seg(lo, n):
        return _dot(xb, w_ref[:, lo:lo + n])

    if kv_transposed:
        k_ref[...] = _dot_nt(wkvt_ref[0:SW, :], xb)
        v_ref[...] = _dot_nt(wkvt_ref[SW:2 * SW, :], xb)
    else:
        k_ref[...] = _dot_nt(xb, wkvt_ref[0:SW, :])
        v_ref[...] = _dot_nt(xb, wkvt_ref[SW:2 * SW, :])
    pq = seg(0, SW) * (Dh ** -0.5)
    for h in range(H):
        q_ref[h] = pq[:, h * Dh:(h + 1) * Dh].astype(BF16)
    o = SW
    gq = seg(o, KW) * (DK ** -0.5)
    gk = seg(o + KW, KW)
    gv_ref[...] = seg(o + 2 * KW, GW)
    gr_ref[...] = seg(o + 2 * KW + GW, GW)
    gd = _dot(xb, wgd_ref[...]).astype(BF16)
    u = _dot(gd, wup_ref[...]) + bg_ref[...]
    lg = -_softplus(-u) * (1.0 / GLA_GATE_NORM)
    for h in range(GH):
        sl = slice(h * DK, (h + 1) * DK)
        gq_ref[h] = gq[:, sl]
        gk_ref[h] = gk[:, sl]
        lg_ref[h] = lg[:, sl]


def _in_proj(x, w_main, w_kvt, w_gd, w_up, b_gate, dims, seq_len=None, layer=0, depth=1, kv_prev=()):
    H, Dh, GH, DK, DV = dims
    M, D = x.shape
    SW, KW, GW = H * Dh, GH * DK, GH * DV
    tm = _tile(seq_len or M, ROW_TILE)
    row = lambda n: pl.BlockSpec((tm, n), lambda i: (i, 0))
    hm = lambda nh, n: pl.BlockSpec((nh, tm, n), lambda i: (0, i, 0))
    full = lambda a: pl.BlockSpec(a.shape, lambda i: (0,) * a.ndim)
    if seq_len:
        per_seq = seq_len // tm
        n_seq = M // seq_len
        kv_shape = jax.ShapeDtypeStruct((depth * n_seq, SW, seq_len), F32)
        kv_spec = pl.BlockSpec((None, SW, tm), lambda i: (layer * n_seq + i // per_seq, 0, i % per_seq))
    else:
        kv_shape, kv_spec = jax.ShapeDtypeStruct((M, SW), F32), row(SW)
    out_shape = (
        jax.ShapeDtypeStruct((H, M, Dh), BF16), kv_shape, kv_shape,
        jax.ShapeDtypeStruct((GH, M, DK), F32), jax.ShapeDtypeStruct((GH, M, DK), F32),
        jax.ShapeDtypeStruct((GH, M, DK), F32),
        jax.ShapeDtypeStruct((M, GW), F32), jax.ShapeDtypeStruct((M, GW), F32),
    )
    out_specs = (hm(H, Dh), kv_spec, kv_spec, hm(GH, DK), hm(GH, DK), hm(GH, DK), row(GW), row(GW))
    return pl.pallas_call(
        functools.partial(_in_proj_kernel, H=H, Dh=Dh, GH=GH, DK=DK, GW=GW, kv_transposed=bool(seq_len),
                          n_alias=len(kv_prev)),
        out_shape=out_shape,
        grid=(M // tm,),
        in_specs=[row(D), full(w_main), full(w_kvt), full(w_gd), full(w_up), full(b_gate)]
        + [pl.BlockSpec(memory_space=pl.ANY)] * len(kv_prev),
        out_specs=out_specs,
        input_output_aliases={6 + n: 1 + n for n in range(len(kv_prev))},
        compiler_params=_cparams(("parallel",)),
        name="in_proj",
    )(x, w_main, w_kvt, w_gd, w_up, b_gate, *kv_prev)


def _sb_blocks(items, u_ref, Rs, accs):
    zs = [_dot(it["q"], it["kt"]) + it["bias"] for it in items]
    sps = [_softplus(z) if it["mask"] is None else jnp.where(it["mask"], _softplus(z), 0.0)
           for it, z in zip(items, zs)]
    parts = [_split(sp) for sp in sps]
    S = [None] * len(items)
    for w in sorted({z.shape[1] for z in zs}):
        idx = [n for n, z in enumerate(zs) if z.shape[1] == w]
        stack = jnp.concatenate([jnp.concatenate(parts[n], axis=1) for n in idx], axis=0)
        uu = jnp.concatenate([u_ref[0:w, 0:w]] * 2, axis=0)
        out = _dot(stack, uu)
        o = 0
        for n in idx:
            nr = zs[n].shape[0]
            S[n] = out[o:o + nr]
            o += nr
    Rs, accs = list(Rs), list(accs)
    for n, it in enumerate(items):
        r = it["row"]
        logw = zs[n] - S[n] - Rs[r]
        if it["mask"] is not None:
            logw = jnp.where(it["mask"], logw, -1e30)
        accs[r] = accs[r] + _dot_nt(jnp.exp(logw).astype(BF16), it["vt"])
        Rs[r] = Rs[r] + S[n][:, 0:1]
    return Rs, accs


def _sb_prompt_kernel(qi_ref, kj_ref, bias_ref, q_ref, kt_ref, vt_ref, g_ref, u_ref, o_ref, r_scr, acc_scr,
                      *, H, Dh, T, SUB, KC):
    p = pl.program_id(1)
    i, j = qi_ref[p], kj_ref[p]
    nsub = T // SUB
    HG = SB_HEADS_PER_ITER

    def run(diag):
        def head_group(hg, carry):
            rs = [slice(r * SUB, (r + 1) * SUB) for r in range(nsub)]
            items, Rs, accs = [], [], []
            for hh in range(HG):
                h = hg * HG + hh
                bias = bias_ref[h]
                hs = pl.ds(pl.multiple_of(h * Dh, Dh), Dh)
                for r in range(nsub):
                    if diag:
                        Rs.append(jnp.zeros((SUB, 1), F32))
                        accs.append(jnp.zeros((SUB, Dh), F32))
                    else:
                        Rs.append(r_scr[h, rs[r], :])
                        accs.append(acc_scr[h, rs[r], :])
                    q = q_ref[h, rs[r], :]
                    c1 = (r + 1) * SUB if diag else T
                    while c1 > 0:
                        c0 = ((c1 - 1) // KC) * KC
                        mask = None
                        if diag and c1 > r * SUB:
                            rows = lax.broadcasted_iota(jnp.int32, (SUB, c1 - c0), 0) + r * SUB
                            cols = lax.broadcasted_iota(jnp.int32, (SUB, c1 - c0), 1) + c0
                            mask = cols < rows
                        items.append(dict(row=hh * nsub + r, q=q, kt=kt_ref[hs, c0:c1].astype(BF16),
                                          vt=vt_ref[hs, c0:c1].astype(BF16), bias=bias, mask=mask))
                        c1 = c0
            Rs, accs = _sb_blocks(items, u_ref, Rs, accs)
            for hh in range(HG):
                h = hg * HG + hh
                for r in range(nsub):
                    r_scr[h, rs[r], :] = Rs[hh * nsub + r]
                    acc_scr[h, rs[r], :] = accs[hh * nsub + r]
            return carry

        lax.fori_loop(0, H // HG, head_group, 0)

    @pl.when(i == j)
    def _():
        run(True)

    @pl.when(i != j)
    def _():
        run(False)

    @pl.when(j == 0)
    def _():
        def normed(h):
            o = acc_scr[h]
            ms = jnp.mean(o * o, axis=-1, keepdims=True)
            return o * lax.rsqrt(ms + RMS_EPS) * g_ref[h]

        for hp in range(H // 2):
            o_ref[:, 2 * hp * Dh:(2 * hp + 2) * Dh] = jnp.concatenate(
                [normed(2 * hp), normed(2 * hp + 1)], axis=-1).astype(BF16)


def _sb_prompt(q_hm, kt, vt, bias, g_hm, seq0=0):
    H, M, Dh = q_hm.shape
    _, SW, L = kt.shape
    B = M // L
    T = _tile(L, SB_TILE)
    SUB = _tile(T, SB_SUB)
    KC = _tile(T, SB_KEY_CHUNK)
    nq = L // T
    assert H % SB_HEADS_PER_ITER == 0
    pairs = [(i, j) for i in range(nq) for j in range(i, -1, -1)]
    qi = jnp.array([p[0] for p in pairs], jnp.int32)
    kj = jnp.array([p[1] for p in pairs], jnp.int32)
    u = (jnp.arange(KC)[:, None] >= jnp.arange(KC)[None, :]).astype(BF16)
    kv_spec = pl.BlockSpec((None, SW, T), lambda b, p, qi, kj: (seq0 + b, 0, kj[p]))
    grid_spec = pltpu.PrefetchScalarGridSpec(
        num_scalar_prefetch=2,
        grid=(B, len(pairs)),
        in_specs=[
            pl.BlockSpec(memory_space=pltpu.SMEM),
            pl.BlockSpec((H, T, Dh), lambda b, p, qi, kj: (0, b * nq + qi[p], 0)),
            kv_spec, kv_spec,
            pl.BlockSpec((H, 1, Dh), lambda b, p, qi, kj: (0, 0, 0)),
            pl.BlockSpec((KC, KC), lambda b, p, qi, kj: (0, 0)),
        ],
        out_specs=pl.BlockSpec((T, SW), lambda b, p, qi, kj: (b * nq + qi[p], 0)),
        scratch_shapes=[pltpu.VMEM((H, T, 1), F32), pltpu.VMEM((H, T, Dh), F32)],
    )
    return pl.pallas_call(
        functools.partial(_sb_prompt_kernel, H=H, Dh=Dh, T=T, SUB=SUB, KC=KC),
        out_shape=jax.ShapeDtypeStruct((M, SW), BF16),
        grid_spec=grid_spec,
        compiler_params=_cparams(("parallel", "arbitrary")),
        name="sb_prompt",
    )(qi, kj, bias, q_hm, kt, vt, g_hm, u)


def _sb_decode_kernel(pt_ref, qbd_ref, kn_ref, vn_ref, bias_ref, g_ref, own_ref, u_ref, ck_hbm, cv_hbm,
                      o_ref, kbuf, vbuf, sem, acc_scr, r_scr, *, layer, G, Ld, H, Dh):
    b, s = pl.program_id(0), pl.program_id(1)
    nb, ns = pl.num_programs(0), pl.num_programs(1)
    NP = ns * G
    HQ = Ld * H
    P = u_ref.shape[0]
    t = b * ns + s
    slot = lax.rem(t, 2)

    def page_copies(bb, ss, sl):
        cps = []
        for i in range(G):
            page = pt_ref[bb, NP - 1 - (ss * G + i)]
            cps.append(pltpu.make_async_copy(ck_hbm.at[layer, page], kbuf.at[sl, i], sem.at[sl]))
            cps.append(pltpu.make_async_copy(cv_hbm.at[layer, page], vbuf.at[sl, i], sem.at[sl]))
        return cps

    @pl.when(t == 0)
    def _():
        for cp in page_copies(b, s, slot):
            cp.start()

    @pl.when(t + 1 < nb * ns)
    def _():
        wrap = s + 1 == ns
        for cp in page_copies(jnp.where(wrap, b + 1, b), jnp.where(wrap, 0, s + 1), 1 - slot):
            cp.start()

    for cp in page_copies(b, s, slot):
        cp.wait()

    q = qbd_ref[...]
    bias = bias_ref[...]

    @pl.when(s == 0)
    def _():
        rows = lax.broadcasted_iota(jnp.int32, (HQ, P), 0)
        cols = lax.broadcasted_iota(jnp.int32, (HQ, P), 1)
        mask = rows >= cols * H + H
        Rs, accs = _sb_blocks([dict(row=0, q=q, kt=kn_ref[...], vt=vn_ref[...], bias=bias, mask=mask)], u_ref,
                              [jnp.zeros((HQ, 1), F32)], [jnp.zeros((HQ, H * Dh), F32)])
        r_scr[...] = Rs[0]
        acc_scr[...] = accs[0]

    items = [dict(row=0, q=q, kt=kbuf[slot, i].astype(BF16), vt=vbuf[slot, i].astype(BF16), bias=bias, mask=None)
             for i in range(G)]
    Rs, accs = _sb_blocks(items, u_ref, [r_scr[...]], [acc_scr[...]])
    r_scr[...] = Rs[0]
    acc_scr[...] = accs[0]

    @pl.when(s == pl.num_programs(1) - 1)
    def _():
        o = acc_scr[...] * own_ref[...]
        ms = jnp.sum(o * o, axis=-1, keepdims=True) * (1.0 / Dh)
        o = o * lax.rsqrt(ms + RMS_EPS)
        g = g_ref[...]
        for t in range(Ld):
            o_ref[t:t + 1, :] = jnp.sum(o[t * H:(t + 1) * H, :], axis=0, keepdims=True) * g


def _sb_decode(layer, qbd, kt_new, vt_new, bias_col, g, cache_kt, cache_vt, page_table, Ld, H, Dh):
    Bd, HQ, SW = qbd.shape
    NP = page_table.shape[1]
    P = cache_kt.shape[3]
    G = DEC_PAGES_PER_STEP
    while NP % G:
        G //= 2
    u = (jnp.arange(P)[:, None] >= jnp.arange(P)[None, :]).astype(BF16)
    own = (jnp.arange(HQ)[:, None] % H == jnp.arange(SW)[None, :] // Dh).astype(F32)

    per_b = lambda r, c: pl.BlockSpec((None, r, c), lambda b, s, pt: (b, 0, 0))
    const = lambda a: pl.BlockSpec(a.shape, lambda b, s, pt: (0,) * a.ndim)
    hbm = pl.BlockSpec(memory_space=pl.ANY)
    grid_spec = pltpu.PrefetchScalarGridSpec(
        num_scalar_prefetch=1,
        grid=(Bd, NP // G),
        in_specs=[per_b(HQ, SW), per_b(SW, P), per_b(SW, P), const(bias_col), const(g), const(own), const(u),
                  hbm, hbm],
        out_specs=per_b(Ld, SW),
        scratch_shapes=[pltpu.VMEM((2, G, SW, P), F32), pltpu.VMEM((2, G, SW, P), F32),
                        pltpu.SemaphoreType.DMA((2,)),
                        pltpu.VMEM((HQ, SW), F32), pltpu.VMEM((HQ, 1), F32)],
    )
    return pl.pallas_call(
        functools.partial(_sb_decode_kernel, layer=layer, G=G, Ld=Ld, H=H, Dh=Dh),
        out_shape=jax.ShapeDtypeStruct((Bd, Ld, SW), F32),
        grid_spec=grid_spec,
        compiler_params=_cparams(("arbitrary", "arbitrary")),
        name="sb_decode",
    )(page_table, qbd, kt_new, vt_new, bias_col, g, own, u, cache_kt, cache_vt)


def _gla_kernel(q_ref, k_ref, lg_ref, v_ref, r_ref, s0_ref, g_ref, ltri2_ref, cmask_ref, o_ref, sout_ref, st_scr,
                *, C, nC, HPS):
    c = pl.program_id(2)

    @pl.when(c == 0)
    def _():
        st_scr[...] = s0_ref[...]

    cs = [slice(ci * C, (ci + 1) * C) for ci in range(nC)]
    ltri2 = ltri2_ref[...]
    causal = cmask_ref[...] > 0.5
    DV = st_scr.shape[1]
    for hh in range(HPS):
        vs = slice(hh * DV, (hh + 1) * DV)
        q, k, lg = q_ref[hh], k_ref[hh], lg_ref[hh]
        vb = v_ref[:, vs].astype(BF16)
        hi, lo = _split(lg)
        b = jnp.concatenate([_dot(ltri2, jnp.concatenate([hi[sl], lo[sl]], axis=0)) for sl in cs], axis=0)
        b_last = [b[(ci + 1) * C - 1:(ci + 1) * C, :] for ci in range(nC)]
        b_end = jnp.concatenate([jnp.broadcast_to(bl, (C, bl.shape[1])) for bl in b_last], axis=0)
        a = (q * jnp.exp(b)).astype(BF16)
        kd = (k * jnp.exp(-b)).astype(BF16)
        k2 = (k * jnp.exp(b_end - b)).astype(BF16)
        scs = [jnp.where(causal, _dot_nt(a[sl], kd[sl]), 0.0).astype(BF16) for sl in cs]
        o_intra = jnp.concatenate([_dot(sc, vb[sl]) for sc, sl in zip(scs, cs)], axis=0)
        upd = [_dot_tn(vb[sl], k2[sl]) for sl in cs]
        st = st_scr[hh]
        o_inter = []
        for ci in range(nC):
            o_inter.append(_dot_nt(a[cs[ci]], st.astype(BF16)))
            st = st * jnp.exp(b_last[ci]) + upd[ci]
        st_scr[hh] = st
        o = o_intra + jnp.concatenate(o_inter, axis=0)
        ms = jnp.mean(o * o, axis=-1, keepdims=True)
        o_ref[:, vs] = (o * lax.rsqrt(ms + RMS_EPS) * g_ref[:, vs] * _silu(r_ref[:, vs])).astype(BF16)

    @pl.when(c == pl.num_programs(2) - 1)
    def _():
        sout_ref[...] = st_scr[...]


def _gla(gq, gk, lg, gv, gr, s0t, g, B, L, C, nC):
    GH, M, DK = gq.shape
    DV = gv.shape[1] // GH
    HPS = GLA_HEADS_PER_STEP if GH % GLA_HEADS_PER_STEP == 0 else 1
    blk = C * nC
    nblk = L // blk
    t = jnp.arange(C)
    causal = t[:, None] >= t[None, :]
    ltri2 = jnp.concatenate([causal, causal], axis=1).astype(BF16)
    cmask = causal.astype(F32)
    hm = pl.BlockSpec((HPS, blk, DK), lambda b, h, c: (h, b * nblk + c, 0))
    tokm = pl.BlockSpec((blk, HPS * DV), lambda b, h, c: (b * nblk + c, h))
    state = pl.BlockSpec((None, HPS, DV, DK), lambda b, h, c: (b, h, 0, 0))
    const = lambda a: pl.BlockSpec(a.shape, lambda b, h, c: (0,) * a.ndim)
    return pl.pallas_call(
        functools.partial(_gla_kernel, C=C, nC=nC, HPS=HPS),
        out_shape=(jax.ShapeDtypeStruct((M, GH * DV), BF16), jax.ShapeDtypeStruct((B, GH, DV, DK), F32)),
        grid=(B, GH // HPS, nblk),
        in_specs=[hm, hm, hm, tokm, tokm, state,
                  pl.BlockSpec((1, HPS * DV), lambda b, h, c: (0, h)), const(ltri2), const(cmask)],
        out_specs=(tokm, state),
        scratch_shapes=[pltpu.VMEM((HPS, DV, DK), F32)],
        compiler_params=_cparams(("parallel", "parallel", "arbitrary")),
        name="gla",
    )(gq, gk, lg, gv, gr, s0t, g, ltri2, cmask)


def _merge_ln_kernel(a_ref, b_ref, x_ref, wa_ref, wb_ref, g_ref, beta_ref, o_ref, *, alpha):
    y = _dot(a_ref[...], wa_ref[...]) + _dot(b_ref[...], wb_ref[...])
    o_ref[...] = _layer_norm(alpha * x_ref[...] + y, g_ref[...], beta_ref[...])


def _merge_ln(a, b, x, wa, wb, g, beta, alpha):
    M, D = x.shape
    tm = _tile(M, ROW_TILE)
    row = lambda n: pl.BlockSpec((tm, n), lambda i: (i, 0))
    full = lambda t: pl.BlockSpec(t.shape, lambda i: (0,) * t.ndim)
    return pl.pallas_call(
        functools.partial(_merge_ln_kernel, alpha=alpha),
        out_shape=jax.ShapeDtypeStruct((M, D), F32),
        grid=(M // tm,),
        in_specs=[row(a.shape[1]), row(b.shape[1]), row(D), full(wa), full(wb), full(g), full(beta)],
        out_specs=row(D),
        compiler_params=_cparams(("parallel",)),
        name="merge_ln",
    )(a, b, x, wa, wb, g, beta)


def _ffn_kernel(te_ref, x_ref, wg_ref, wu_ref, wd_ref, *rest, nf, tf, n_tiles, alpha, fuse_ln):
    if fuse_ln:
        g_ref, beta_ref, o_ref, xb_scr, acc_scr = rest
    else:
        o_ref, xb_scr, acc_scr = rest
    i = pl.program_id(0)

    @pl.when(i < te_ref[n_tiles])
    def _():
        xb_scr[...] = x_ref[...].astype(BF16)
        acc_scr[...] = jnp.zeros_like(acc_scr)

        def body(f, carry):
            xb = xb_scr[...]
            cols = pl.ds(pl.multiple_of(f * tf, tf), tf)
            hg = _dot(xb, wg_ref[:, cols])
            hu = _dot(xb, wu_ref[:, cols])
            acc_scr[...] += _dot((_silu(hg) * hu).astype(BF16), wd_ref[cols, :])
            return carry

        lax.fori_loop(0, nf, body, 0)
        if fuse_ln:
            o_ref[...] = _layer_norm(alpha * x_ref[...] + acc_scr[...], g_ref[...], beta_ref[...])
        else:
            o_ref[...] = acc_scr[...]

    @pl.when(i >= te_ref[n_tiles])
    def _():
        o_ref[...] = jnp.zeros_like(o_ref)


def _ffn(x, te, wg, wu, wd, tm, ln=None, alpha=1.0):
    R, D = x.shape
    F = wg.shape[2]
    tf = _tile(F, FF_CHUNK)
    nf = F // tf
    n_tiles = R // tm
    row = pl.BlockSpec((tm, D), lambda i, te: (i, 0))
    w_in = pl.BlockSpec((None, D, F), lambda i, te: (te[i], 0, 0))
    w_out = pl.BlockSpec((None, F, D), lambda i, te: (te[i], 0, 0))
    vec = pl.BlockSpec((1, D), lambda i, te: (0, 0))
    in_specs = [row, w_in, w_in, w_out] + ([vec, vec] if ln else [])
    grid_spec = pltpu.PrefetchScalarGridSpec(
        num_scalar_prefetch=1, grid=(n_tiles,), in_specs=in_specs, out_specs=row,
        scratch_shapes=[pltpu.VMEM((tm, D), BF16), pltpu.VMEM((tm, D), F32)])
    return pl.pallas_call(
        functools.partial(_ffn_kernel, nf=nf, tf=tf, n_tiles=n_tiles, alpha=alpha, fuse_ln=bool(ln)),
        out_shape=jax.ShapeDtypeStruct((R, D), F32),
        grid_spec=grid_spec,
        compiler_params=_cparams(("arbitrary",)),
        name="ffn_ln" if ln else "ffn_experts",
    )(te, x, wg, wu, wd, *(ln or ()))


def _router_kernel(x_ref, wh_ref, wl_ref, b_ref, c0_ref, lstrict_ref, slab_ref, cnt_ref, carry_scr, *, E):
    i = pl.program_id(0)

    @pl.when(i == 0)
    def _():
        carry_scr[...] = c0_ref[...]

    xh, xl = _split(x_ref[...])
    wh, wl = wh_ref[...], wl_ref[...]
    logits = _dot(xh, wh) + (_dot(xl, wh) + _dot(xh, wl)) + b_ref[...]
    tm, W = logits.shape
    lane = lax.broadcasted_iota(jnp.int32, (tm, W), 1).astype(F32)
    neg = -jnp.inf
    lg1 = jnp.where(lane < E, logits, neg)
    m1 = jnp.max(lg1, axis=-1, keepdims=True)
    i1 = jnp.min(jnp.where(lg1 == m1, lane, float(W)), axis=-1, keepdims=True)
    lg2 = jnp.where(lane == i1, neg, lg1)
    m2 = jnp.max(lg2, axis=-1, keepdims=True)
    i2 = jnp.min(jnp.where(lg2 == m2, lane, float(W)), axis=-1, keepdims=True)
    e2 = jnp.exp(m2 - m1)
    g1 = 1.0 / (1.0 + e2)
    g2 = e2 / (1.0 + e2)
    oh = jnp.where((lane == i1) | (lane == i2), 1.0, 0.0)
    before = _dot(lstrict_ref[...], oh.astype(BF16)) + carry_scr[...]
    r1 = jnp.sum(jnp.where(lane == i1, before, 0.0), axis=-1, keepdims=True)
    r2 = jnp.sum(jnp.where(lane == i2, before, 0.0), axis=-1, keepdims=True)
    carry_scr[...] += jnp.sum(oh, axis=0, keepdims=True)
    slab = jnp.zeros((tm, W), F32)
    for col, val in enumerate((i1, i2, g1, g2, r1, r2)):
        slab = jnp.where(lane == col, val, slab)
    slab_ref[...] = slab
    cnt_ref[...] = carry_scr[...]


def _router(x, wh, wl, b, c0, E):
    M, D = x.shape
    tm = _tile(M, ROW_TILE)
    W = wh.shape[1]
    lstrict = (jnp.arange(tm)[:, None] > jnp.arange(tm)[None, :]).astype(BF16)
    full = lambda t: pl.BlockSpec(t.shape, lambda i: (0,) * t.ndim)
    return pl.pallas_call(
        functools.partial(_router_kernel, E=E),
        out_shape=(jax.ShapeDtypeStruct((M, W), F32), jax.ShapeDtypeStruct((1, W), F32)),
        grid=(M // tm,),
        in_specs=[pl.BlockSpec((tm, D), lambda i: (i, 0)), full(wh), full(wl), full(b), full(c0), full(lstrict)],
        out_specs=(pl.BlockSpec((tm, W), lambda i: (i, 0)), pl.BlockSpec((1, W), lambda i: (0, 0))),
        scratch_shapes=[pltpu.VMEM((1, W), F32)],
        compiler_params=_cparams(("arbitrary",)),
        name="router",
    )(x, wh, wl, b, c0, lstrict)


def _row_copy(src, s, dst, d, sem):
    return pltpu.make_async_copy(src.at[pl.ds(s, 1)], dst.at[pl.ds(d, 1)], sem)


def _dispatch_kernel(dest_ref, x_ref, xs_in, xs_out, sem, *, tm):
    del xs_in

    def start(r, carry):
        for j in range(TOP_K):
            _row_copy(x_ref, r, xs_out, dest_ref[0, 0, TOP_K * r + j], sem).start()
        return carry

    lax.fori_loop(0, tm, start, 0, unroll=ROW_DMA_UNROLL)
    for j in range(TOP_K):
        pltpu.make_async_copy(x_ref, xs_out.at[pl.ds(0, tm)], sem).wait()


def _dispatch(x, dest3, xs):
    M, D = x.shape
    tm = dest3.shape[2] // TOP_K
    return pl.pallas_call(
        functools.partial(_dispatch_kernel, tm=tm),
        out_shape=jax.ShapeDtypeStruct(xs.shape, xs.dtype),
        grid=(M // tm,),
        in_specs=[pl.BlockSpec((1, 1, TOP_K * tm), lambda i: (i, 0, 0), memory_space=pltpu.SMEM),
                  pl.BlockSpec((tm, D), lambda i: (i, 0)), pl.BlockSpec(memory_space=pl.ANY)],
        out_specs=pl.BlockSpec(memory_space=pl.ANY),
        scratch_shapes=[pltpu.SemaphoreType.DMA],
        input_output_aliases={2: 0},
        compiler_params=_cparams(("arbitrary",)),
        name="moe_dispatch",
    )(dest3, x, xs)


def _combine_kernel(dest_ref, slab_ref, x_ref, g_ref, beta_ref, ys_hbm, o_ref, buf, sem, *, tm, alpha):
    def start(r, carry):
        for j in range(TOP_K):
            _row_copy(ys_hbm, dest_ref[0, 0, TOP_K * r + j], buf.at[j], r, sem).start()
        return carry

    lax.fori_loop(0, tm, start, 0, unroll=ROW_DMA_UNROLL)
    for j in range(TOP_K):
        pltpu.make_async_copy(ys_hbm.at[pl.ds(0, tm)], buf.at[j], sem).wait()
    slab = slab_ref[...]
    y = slab[:, 2:3] * buf[0] + slab[:, 3:4] * buf[1]
    o_ref[...] = _layer_norm(alpha * x_ref[...] + y, g_ref[...], beta_ref[...])


def _combine(x, slab, dest3, ys, g, beta, alpha):
    M, D = x.shape
    tm = dest3.shape[2] // TOP_K
    row = lambda n: pl.BlockSpec((tm, n), lambda i: (i, 0))
    vec = pl.BlockSpec((1, D), lambda i: (0, 0))
    return pl.pallas_call(
        functools.partial(_combine_kernel, tm=tm, alpha=alpha),
        out_shape=jax.ShapeDtypeStruct((M, D), F32),
        grid=(M // tm,),
        in_specs=[pl.BlockSpec((1, 1, TOP_K * tm), lambda i: (i, 0, 0), memory_space=pltpu.SMEM),
                  row(slab.shape[1]), row(D), vec, vec, pl.BlockSpec(memory_space=pl.ANY)],
        out_specs=row(D),
        scratch_shapes=[pltpu.VMEM((TOP_K, tm, D), F32), pltpu.SemaphoreType.DMA],
        compiler_params=_cparams(("arbitrary",)),
        name="moe_combine",
    )(dest3, slab, x, g, beta, ys)


def _ffn_weights(wg, wu, wd):
    return wg.astype(BF16), wu.astype(BF16), wd.astype(BF16)


def _moe(xp, xs, router_w, router_b, wg, wu, wd, ln_g, ln_b, alpha):
    E = router_w.shape[1]
    D = xp.shape[1]
    pad = ROUTER_PAD - E
    rw = jnp.pad(router_w, ((0, 0), (0, pad)))
    rwh = rw.astype(BF16)
    rwl = (rw - rwh.astype(F32)).astype(BF16)
    rb = jnp.pad(router_b, (0, pad))[None, :]
    slab_p, cnt_p = _router(xp, rwh, rwl, rb, jnp.zeros((1, ROUTER_PAD), F32), E)
    slab_s, cnt = _router(xs, rwh, rwl, rb, cnt_p, E)

    n_assign = TOP_K * (xp.shape[0] + xs.shape[0])
    tm = ROW_TILE
    n_tiles = -(-n_assign // tm) + E
    counts = cnt[0, :E].astype(jnp.int32)
    padded = ((counts + tm - 1) // tm) * tm
    off_end = jnp.cumsum(padded)
    off_start = off_end - padded
    tile_row0 = jnp.arange(n_tiles, dtype=jnp.int32) * tm
    tile_expert = jnp.minimum(jnp.sum(tile_row0[:, None] >= off_end[None, :], axis=1), E - 1)
    te = jnp.concatenate([tile_expert, off_end[-1:] // tm]).astype(jnp.int32)

    def dests(slab):
        e = slab[:, 0:TOP_K].astype(jnp.int32)
        rank = slab[:, 4:4 + TOP_K].astype(jnp.int32)
        d = off_start[e] + rank
        t = _tile(slab.shape[0], ROW_TILE)
        return d.reshape(slab.shape[0] // t, 1, TOP_K * t)

    dest_p, dest_s = dests(slab_p), dests(slab_s)
    xsort = jnp.zeros((n_tiles * tm, D), F32)
    xsort = _dispatch(xp, dest_p, xsort)
    xsort = _dispatch(xs, dest_s, xsort)
    ys = _ffn(xsort, te, wg, wu, wd, tm)
    return (_combine(xp, slab_p, dest_p, ys, ln_g, ln_b, alpha),
            _combine(xs, slab_s, dest_s, ys, ln_g, ln_b, alpha))


def kernel(x_prompt, x_sample, cache_k, cache_v, state_gla, page_table, w_in, w_gate_up, b_gate, sb_bias,
           sb_norm_g, gla_norm_g, w_o, ln1_g, ln1_b, ln2_g, ln2_b, ffn_w_gate, ffn_w_up, ffn_w_down,
           router_w, router_b, moe_w_gate, moe_w_up, moe_w_down):
    B, L, D = x_prompt.shape
    Bd, Ld, _ = x_sample.shape
    depth, n_pool, P, H, Dh = cache_k.shape
    _, _, GH, DK, DV = state_gla.shape
    assert Dh == 64 and DK == 64, "1/sqrt(head dim) must be a power of two to fold into bf16 operands"
    SW, KW, GW = H * Dh, GH * DK, GH * DV
    n_main = 3 * SW + 2 * KW + 2 * GW
    rank = w_gate_up.shape[1]
    alpha = (2 * depth) ** 0.25
    dims = (H, Dh, GH, DK, DV)
    Ls = -(-Ld // GLA_SAMPLE_CHUNK) * GLA_SAMPLE_CHUNK

    xp = x_prompt.reshape(B * L, D)
    xs = x_sample.reshape(Bd * Ld, D)
    cache_kt = cache_k.transpose(0, 1, 3, 4, 2).reshape(depth, n_pool, SW, P)
    cache_vt = cache_v.transpose(0, 1, 3, 4, 2).reshape(depth, n_pool, SW, P)
    head_eye = jnp.eye(H, dtype=BF16)
    row2 = lambda v: v[None, :]
    outs = [[] for _ in range(6)]
    kv_prompt = (jnp.zeros((depth * B, SW, L), F32), jnp.zeros((depth * B, SW, L), F32))

    for l in range(depth):
        w_main = jnp.concatenate([w_in[l, :, :SW], w_in[l, :, 3 * SW:n_main]], axis=1).astype(BF16)
        w_kvt = w_in[l, :, SW:3 * SW].T.astype(BF16)
        w_gd = jnp.pad(w_in[l, :, n_main:], ((0, 0), (0, GLA_GATE_PAD - rank))).astype(BF16)
        w_up = jnp.pad(w_gate_up[l], ((0, GLA_GATE_PAD - rank), (0, 0))).astype(BF16)
        bg = row2(b_gate[l])
        wo = w_o[l].astype(BF16)
        g_sb_hm = sb_norm_g[l].reshape(H, 1, Dh)
        g_gla = row2(gla_norm_g[l])

        q, kt, vt, gq, gk, lg, gv, gr = _in_proj(xp, w_main, w_kvt, w_gd, w_up, bg, dims, seq_len=L,
                                                 layer=l, depth=depth, kv_prev=kv_prompt)
        kv_prompt = (kt, vt)
        a = _sb_prompt(q, kt, vt, sb_bias[l], g_sb_hm, seq0=l * B)
        C = _tile(L, GLA_CHUNK)
        nC = GLA_CHUNKS_PER_STEP
        while (L // C) % nC:
            nC //= 2
        bo, stp = _gla(gq, gk, lg, gv, gr, jnp.zeros((B, GH, DV, DK), F32), g_gla, B, L, C, nC)
        xp = _merge_ln(a, bo, xp, wo[:SW], wo[SW:], row2(ln1_g[l]), row2(ln1_b[l]), alpha)
        outs[2].append(jnp.swapaxes(stp, -1, -2))

        q, kf, vf, gq, gk, lg, gv, gr = _in_proj(xs, w_main, w_kvt, w_gd, w_up, bg, dims)
        q4 = q.reshape(H, Bd, Ld, Dh).transpose(1, 2, 0, 3)
        qbd = (q4[:, :, :, None, :] * head_eye[None, None, :, :, None]).reshape(Bd, Ld * H, SW)
        pad_new = lambda t: jnp.pad(jnp.swapaxes(t.reshape(Bd, Ld, SW), 1, 2).astype(BF16),
                                    ((0, 0), (0, 0), (0, P - Ld)))
        bias_col = jnp.tile(sb_bias[l], Ld)[:, None]
        a = _sb_decode(l, qbd, pad_new(kf), pad_new(vf), bias_col, row2(sb_norm_g[l]),
                       cache_kt, cache_vt, page_table, Ld, H, Dh)
        a = a.reshape(Bd * Ld, SW).astype(BF16)
        pad_hm = lambda t: jnp.pad(t.reshape(GH, Bd, Ld, DK), ((0, 0), (0, 0), (0, Ls - Ld), (0, 0))
                                   ).reshape(GH, Bd * Ls, DK)
        pad_tm = lambda t: jnp.pad(t.reshape(Bd, Ld, GW), ((0, 0), (0, Ls - Ld), (0, 0))).reshape(Bd * Ls, GW)
        bo, sts = _gla(pad_hm(gq), pad_hm(gk), pad_hm(lg), pad_tm(gv), pad_tm(gr),
                       jnp.swapaxes(state_gla[l], -1, -2), g_gla, Bd, Ls, Ls, 1)
        bo = bo.reshape(Bd, Ls, GW)[:, :Ld].reshape(Bd * Ld, GW)
        xs = _merge_ln(a, bo, xs, wo[:SW], wo[SW:], row2(ln1_g[l]), row2(ln1_b[l]), alpha)
        outs[3].append(kf.reshape(Bd, Ld, H, Dh))
        outs[4].append(vf.reshape(Bd, Ld, H, Dh))
        outs[5].append(jnp.swapaxes(sts, -1, -2))

        i = l // 2
        ln = (row2(ln2_g[l]), row2(ln2_b[l]))
        if l % 2 == 0:
            wg, wu, wd = _ffn_weights(ffn_w_gate[i][None], ffn_w_up[i][None], ffn_w_down[i][None])
            for_x = lambda x: _ffn(x, jnp.array([0] * (x.shape[0] // _tile(x.shape[0], ROW_TILE))
                                                + [x.shape[0] // _tile(x.shape[0], ROW_TILE)], jnp.int32),
                                   wg, wu, wd, _tile(x.shape[0], ROW_TILE), ln=ln, alpha=alpha)
            xp, xs = for_x(xp), for_x(xs)
        else:
            wg, wu, wd = _ffn_weights(moe_w_gate[i], moe_w_up[i], moe_w_down[i])
            xp, xs = _moe(xp, xs, router_w[i], router_b[i], wg, wu, wd, ln[0], ln[1], alpha)

    stack = lambda rows: jnp.stack(rows, axis=0)
    kv_out = lambda t: t.reshape(depth, B, H, Dh, L).transpose(0, 1, 4, 2, 3)
    return (xp.reshape(B, L, D), xs.reshape(Bd, Ld, D), kv_out(kv_prompt[0]), kv_out(kv_prompt[1]), stack(outs[2]),
            stack(outs[3]), stack(outs[4]), stack(outs[5]))
```

```python
import functools

import jax
import jax.numpy as jnp
from jax import lax
from jax.experimental import pallas as pl
from jax.experimental.pallas import tpu as pltpu

F32 = jnp.float32
BF16 = jnp.bfloat16

LN_EPS = 1e-5
RMS_EPS = 1e-6
GLA_GATE_NORM = 16.0
GLA_GATE_PAD = 128
ROUTER_PAD = 128
TOP_K = 2
VMEM_LIMIT = 56 * 1024 * 1024

ROW_TILE = 512
SB_TILE = 512
SB_SUB = 128
SB_HEADS_PER_ITER = 4
SB_KEY_CHUNK = 256
GLA_CHUNK = 64
GLA_CHUNKS_PER_STEP = 8
GLA_SAMPLE_CHUNK = 16
GLA_HEADS_PER_STEP = 4
FF_CHUNK = 4096
DEC_PAGES_PER_STEP = 16
ROW_DMA_UNROLL = 8


def _tile(n, pref):
    t = min(n, pref)
    while n % t:
        t -= 8
    assert t > 0 and (t % 8 == 0 or t == n)
    return t


def _cparams(sem):
    return pltpu.CompilerParams(dimension_semantics=sem, vmem_limit_bytes=VMEM_LIMIT)


def _softplus(z):
    return jnp.maximum(z, 0.0) + jnp.log(1.0 + jnp.exp(-jnp.abs(z)))


def _split(x):
    hi = x.astype(BF16)
    lo = (x - hi.astype(F32)).astype(BF16)
    return hi, lo


def _dot(a, b):
    return jnp.dot(a, b, preferred_element_type=F32)


def _dot_nt(a, b):
    return lax.dot_general(a, b, (((1,), (1,)), ((), ())), preferred_element_type=F32)


def _dot_tn(a, b):
    return lax.dot_general(a, b, (((0,), (0,)), ((), ())), preferred_element_type=F32)


def _layer_norm(x, g, b):
    mu = jnp.mean(x, axis=-1, keepdims=True)
    xc = x - mu
    var = jnp.mean(xc * xc, axis=-1, keepdims=True)
    return xc * lax.rsqrt(var + LN_EPS) * g + b


def _silu(x):
    return x / (1.0 + jnp.exp(-x))


def _in_proj_kernel(*refs, H, Dh, GH, DK, GW, kv_transposed, n_alias):
    x_ref, w_ref, wkvt_ref, wgd_ref, wup_ref, bg_ref = refs[:6]
    q_ref, k_ref, v_ref, gq_ref, gk_ref, lg_ref, gv_ref, gr_ref = refs[6 + n_alias:]
    xb = x_ref[...].astype(BF16)
    SW = H * Dh
    KW = GH * DK

    def seg(lo, n):
        return _dot(xb, w_ref[:, lo:lo + n])

    if kv_transposed:
        k_ref[...] = _dot_nt(wkvt_ref[0:SW, :], xb)
        v_ref[...] = _dot_nt(wkvt_ref[SW:2 * SW, :], xb)
    else:
        k_ref[...] = _dot_nt(xb, wkvt_ref[0:SW, :])
        v_ref[...] = _dot_nt(xb, wkvt_ref[SW:2 * SW, :])
    pq = seg(0, SW) * (Dh ** -0.5)
    for h in range(H):
        q_ref[h] = pq[:, h * Dh:(h + 1) * Dh].astype(BF16)
    o = SW
    gq = seg(o, KW) * (DK ** -0.5)
    gk = seg(o + KW, KW)
    gv_ref[...] = seg(o + 2 * KW, GW)
    gr_ref[...] = seg(o + 2 * KW + GW, GW)
    gd = _dot(xb, wgd_ref[...]).astype(BF16)
    u = _dot(gd, wup_ref[...]) + bg_ref[...]
    lg = -_softplus(-u) * (1.0 / GLA_GATE_NORM)
    for h in range(GH):
        sl = slice(h * DK, (h + 1) * DK)
        gq_ref[h] = gq[:, sl]
        gk_ref[h] = gk[:, sl]
        lg_ref[h] = lg[:, sl]


def _in_proj(x, w_main, w_kvt, w_gd, w_up, b_gate, dims, seq_len=None, layer=0, depth=1, kv_prev=()):
    H, Dh, GH, DK, DV = dims
    M, D = x.shape
    SW, KW, GW = H * Dh, GH * DK, GH * DV
    tm = _tile(seq_len or M, ROW_TILE)
    row = lambda n: pl.BlockSpec((tm, n), lambda i: (i, 0))
    hm = lambda nh, n: pl.BlockSpec((nh, tm, n), lambda i: (0, i, 0))
    full = lambda a: pl.BlockSpec(a.shape, lambda i: (0,) * a.ndim)
    if seq_len:
        per_seq = seq_len // tm
        n_seq = M // seq_len
        kv_shape = jax.ShapeDtypeStruct((depth * n_seq, SW, seq_len), F32)
        kv_spec = pl.BlockSpec((None, SW, tm), lambda i: (layer * n_seq + i // per_seq, 0, i % per_seq))
    else:
        kv_shape, kv_spec = jax.ShapeDtypeStruct((M, SW), F32), row(SW)
    out_shape = (
        jax.ShapeDtypeStruct((H, M, Dh), BF16), kv_shape, kv_shape,
        jax.ShapeDtypeStruct((GH, M, DK), F32), jax.ShapeDtypeStruct((GH, M, DK), F32),
        jax.ShapeDtypeStruct((GH, M, DK), F32),
        jax.ShapeDtypeStruct((M, GW), F32), jax.ShapeDtypeStruct((M, GW), F32),
    )
    out_specs = (hm(H, Dh), kv_spec, kv_spec, hm(GH, DK), hm(GH, DK), hm(GH, DK), row(GW), row(GW))
    return pl.pallas_call(
        functools.partial(_in_proj_kernel, H=H, Dh=Dh, GH=GH, DK=DK, GW=GW, kv_transposed=bool(seq_len),
                          n_alias=len(kv_prev)),
        out_shape=out_shape,
        grid=(M // tm,),
        in_specs=[row(D), full(w_main), full(w_kvt), full(w_gd), full(w_up), full(b_gate)]
        + [pl.BlockSpec(memory_space=pl.ANY)] * len(kv_prev),
        out_specs=out_specs,
        input_output_aliases={6 + n: 1 + n for n in range(len(kv_prev))},
        compiler_params=_cparams(("parallel",)),
        name="in_proj",
    )(x, w_main, w_kvt, w_gd, w_up, b_gate, *kv_prev)


def _sb_blocks(items, u_ref, Rs, accs):
    zs = [_dot(it["q"], it["kt"]) + it["bias"] for it in items]
    sps = [_softplus(z) if it["mask"] is None else jnp.where(it["mask"], _softplus(z), 0.0)
           for it, z in zip(items, zs)]
    parts = [_split(sp) for sp in sps]
    S = [None] * len(items)
    for w in sorted({z.shape[1] for z in zs}):
        idx = [n for n, z in enumerate(zs) if z.shape[1] == w]
        stack = jnp.concatenate([jnp.concatenate(parts[n], axis=1) for n in idx], axis=0)
        uu = jnp.concatenate([u_ref[0:w, 0:w]] * 2, axis=0)
        out = _dot(stack, uu)
        o = 0
        for n in idx:
            nr = zs[n].shape[0]
            S[n] = out[o:o + nr]
            o += nr
    Rs, accs = list(Rs), list(accs)
    for n, it in enumerate(items):
        r = it["row"]
        logw = zs[n] - S[n] - Rs[r]
        if it["mask"] is not None:
            logw = jnp.where(it["mask"], logw, -1e30)
        accs[r] = accs[r] + _dot_nt(jnp.exp(logw).astype(BF16), it["vt"])
        Rs[r] = Rs[r] + S[n][:, 0:1]
    return Rs, accs


def _sb_prompt_kernel(qi_ref, kj_ref, bias_ref, q_ref, kt_ref, vt_ref, g_ref, u_ref, o_ref, r_scr, acc_scr,
                      *, H, Dh, T, SUB, KC):
    p = pl.program_id(1)
    i, j = qi_ref[p], kj_ref[p]
    nsub = T // SUB
    HG = SB_HEADS_PER_ITER

    def run(diag):
        def head_group(hg, carry):
            rs = [slice(r * SUB, (r + 1) * SUB) for r in range(nsub)]
            items, Rs, accs = [], [], []
            for hh in range(HG):
                h = hg * HG + hh
                bias = bias_ref[h]
                hs = pl.ds(pl.multiple_of(h * Dh, Dh), Dh)
                for r in range(nsub):
                    if diag:
                        Rs.append(jnp.zeros((SUB, 1), F32))
                        accs.append(jnp.zeros((SUB, Dh), F32))
                    else:
                        Rs.append(r_scr[h, rs[r], :])
                        accs.append(acc_scr[h, rs[r], :])
                    q = q_ref[h, rs[r], :]
                    c1 = (r + 1) * SUB if diag else T
                    while c1 > 0:
                        c0 = ((c1 - 1) // KC) * KC
                        mask = None
                        if diag and c1 > r * SUB:
                            rows = lax.broadcasted_iota(jnp.int32, (SUB, c1 - c0), 0) + r * SUB
                            cols = lax.broadcasted_iota(jnp.int32, (SUB, c1 - c0), 1) + c0
                            mask = cols < rows
                        items.append(dict(row=hh * nsub + r, q=q, kt=kt_ref[hs, c0:c1].astype(BF16),
                                          vt=vt_ref[hs, c0:c1].astype(BF16), bias=bias, mask=mask))
                        c1 = c0
            Rs, accs = _sb_blocks(items, u_ref, Rs, accs)
            for hh in range(HG):
                h = hg * HG + hh
                for r in range(nsub):
                    r_scr[h, rs[r], :] = Rs[hh * nsub + r]
                    acc_scr[h, rs[r], :] = accs[hh * nsub + r]
            return carry

        lax.fori_loop(0, H // HG, head_group, 0)

    @pl.when(i == j)
    def _():
        run(True)

    @pl.when(i != j)
    def _():
        run(False)

    @pl.when(j == 0)
    def _():
        def normed(h):
            o = acc_scr[h]
            ms = jnp.mean(o * o, axis=-1, keepdims=True)
            return o * lax.rsqrt(ms + RMS_EPS) * g_ref[h]

        for hp in range(H // 2):
            o_ref[:, 2 * hp * Dh:(2 * hp + 2) * Dh] = jnp.concatenate(
                [normed(2 * hp), normed(2 * hp + 1)], axis=-1).astype(BF16)


def _sb_prompt(q_hm, kt, vt, bias, g_hm, seq0=0):
    H, M, Dh = q_hm.shape
    _, SW, L = kt.shape
    B = M // L
    T = _tile(L, SB_TILE)
    SUB = _tile(T, SB_SUB)
    KC = _tile(T, SB_KEY_CHUNK)
    nq = L // T
    assert H % SB_HEADS_PER_ITER == 0
    pairs = [(i, j) for i in range(nq) for j in range(i, -1, -1)]
    qi = jnp.array([p[0] for p in pairs], jnp.int32)
    kj = jnp.array([p[1] for p in pairs], jnp.int32)
    u = (jnp.arange(KC)[:, None] >= jnp.arange(KC)[None, :]).astype(BF16)
    kv_spec = pl.BlockSpec((None, SW, T), lambda b, p, qi, kj: (seq0 + b, 0, kj[p]))
    grid_spec = pltpu.PrefetchScalarGridSpec(
        num_scalar_prefetch=2,
        grid=(B, len(pairs)),
        in_specs=[
            pl.BlockSpec(memory_space=pltpu.SMEM),
            pl.BlockSpec((H, T, Dh), lambda b, p, qi, kj: (0, b * nq + qi[p], 0)),
            kv_spec, kv_spec,
            pl.BlockSpec((H, 1, Dh), lambda b, p, qi, kj: (0, 0, 0)),
            pl.BlockSpec((KC, KC), lambda b, p, qi, kj: (0, 0)),
        ],
        out_specs=pl.BlockSpec((T, SW), lambda b, p, qi, kj: (b * nq + qi[p], 0)),
        scratch_shapes=[pltpu.VMEM((H, T, 1), F32), pltpu.VMEM((H, T, Dh), F32)],
    )
    return pl.pallas_call(
        functools.partial(_sb_prompt_kernel, H=H, Dh=Dh, T=T, SUB=SUB, KC=KC),
        out_shape=jax.ShapeDtypeStruct((M, SW), BF16),
        grid_spec=grid_spec,
        compiler_params=_cparams(("parallel", "arbitrary")),
        name="sb_prompt",
    )(qi, kj, bias, q_hm, kt, vt, g_hm, u)


def _sb_decode_kernel(pt_ref, qbd_ref, kn_ref, vn_ref, bias_ref, g_ref, own_ref, u_ref, ck_hbm, cv_hbm,
                      o_ref, kbuf, vbuf, sem, acc_scr, r_scr, *, layer, G, Ld, H, Dh):
    b, s = pl.program_id(0), pl.program_id(1)
    nb, ns = pl.num_programs(0), pl.num_programs(1)
    NP = ns * G
    HQ = Ld * H
    P = u_ref.shape[0]
    t = b * ns + s
    slot = lax.rem(t, 2)

    def page_copies(bb, ss, sl):
        cps = []
        for i in range(G):
            page = pt_ref[bb, NP - 1 - (ss * G + i)]
            cps.append(pltpu.make_async_copy(ck_hbm.at[layer, page], kbuf.at[sl, i], sem.at[sl]))
            cps.append(pltpu.make_async_copy(cv_hbm.at[layer, page], vbuf.at[sl, i], sem.at[sl]))
        return cps

    @pl.when(t == 0)
    def _():
        for cp in page_copies(b, s, slot):
            cp.start()

    @pl.when(t + 1 < nb * ns)
    def _():
        wrap = s + 1 == ns
        for cp in page_copies(jnp.where(wrap, b + 1, b), jnp.where(wrap, 0, s + 1), 1 - slot):
            cp.start()

    for cp in page_copies(b, s, slot):
        cp.wait()

    q = qbd_ref[...]
    bias = bias_ref[...]

    @pl.when(s == 0)
    def _():
        rows = lax.broadcasted_iota(jnp.int32, (HQ, P), 0)
        cols = lax.broadcasted_iota(jnp.int32, (HQ, P), 1)
        mask = rows >= cols * H + H
        Rs, accs = _sb_blocks([dict(row=0, q=q, kt=kn_ref[...], vt=vn_ref[...], bias=bias, mask=mask)], u_ref,
                              [jnp.zeros((HQ, 1), F32)], [jnp.zeros((HQ, H * Dh), F32)])
        r_scr[...] = Rs[0]
        acc_scr[...] = accs[0]

    items = [dict(row=0, q=q, kt=kbuf[slot, i].astype(BF16), vt=vbuf[slot, i].astype(BF16), bias=bias, mask=None)
             for i in range(G)]
    Rs, accs = _sb_blocks(items, u_ref, [r_scr[...]], [acc_scr[...]])
    r_scr[...] = Rs[0]
    acc_scr[...] = accs[0]

    @pl.when(s == pl.num_programs(1) - 1)
    def _():
        o = acc_scr[...] * own_ref[...]
        ms = jnp.sum(o * o, axis=-1, keepdims=True) * (1.0 / Dh)
        o = o * lax.rsqrt(ms + RMS_EPS)
        g = g_ref[...]
        for t in range(Ld):
            o_ref[t:t + 1, :] = jnp.sum(o[t * H:(t + 1) * H, :], axis=0, keepdims=True) * g


def _sb_decode(layer, qbd, kt_new, vt_new, bias_col, g, cache_kt, cache_vt, page_table, Ld, H, Dh):
    Bd, HQ, SW = qbd.shape
    NP = page_table.shape[1]
    P = cache_kt.shape[3]
    G = DEC_PAGES_PER_STEP
    while NP % G:
        G //= 2
    u = (jnp.arange(P)[:, None] >= jnp.arange(P)[None, :]).astype(BF16)
    own = (jnp.arange(HQ)[:, None] % H == jnp.arange(SW)[None, :] // Dh).astype(F32)

    per_b = lambda r, c: pl.BlockSpec((None, r, c), lambda b, s, pt: (b, 0, 0))
    const = lambda a: pl.BlockSpec(a.shape, lambda b, s, pt: (0,) * a.ndim)
    hbm = pl.BlockSpec(memory_space=pl.ANY)
    grid_spec = pltpu.PrefetchScalarGridSpec(
        num_scalar_prefetch=1,
        grid=(Bd, NP // G),
        in_specs=[per_b(HQ, SW), per_b(SW, P), per_b(SW, P), const(bias_col), const(g), const(own), const(u),
                  hbm, hbm],
        out_specs=per_b(Ld, SW),
        scratch_shapes=[pltpu.VMEM((2, G, SW, P), F32), pltpu.VMEM((2, G, SW, P), F32),
                        pltpu.SemaphoreType.DMA((2,)),
                        pltpu.VMEM((HQ, SW), F32), pltpu.VMEM((HQ, 1), F32)],
    )
    return pl.pallas_call(
        functools.partial(_sb_decode_kernel, layer=layer, G=G, Ld=Ld, H=H, Dh=Dh),
        out_shape=jax.ShapeDtypeStruct((Bd, Ld, SW), F32),
        grid_spec=grid_spec,
        compiler_params=_cparams(("arbitrary", "arbitrary")),
        name="sb_decode",
    )(page_table, qbd, kt_new, vt_new, bias_col, g, own, u, cache_kt, cache_vt)


def _gla_kernel(q_ref, k_ref, lg_ref, v_ref, r_ref, s0_ref, g_ref, ltri2_ref, cmask_ref, o_ref, sout_ref, st_scr,
                *, C, nC, HPS):
    c = pl.program_id(2)

    @pl.when(c == 0)
    def _():
        st_scr[...] = s0_ref[...]

    cs = [slice(ci * C, (ci + 1) * C) for ci in range(nC)]
    ltri2 = ltri2_ref[...]
    causal = cmask_ref[...] > 0.5
    DV = st_scr.shape[1]
    for hh in range(HPS):
        vs = slice(hh * DV, (hh + 1) * DV)
        q, k, lg = q_ref[hh], k_ref[hh], lg_ref[hh]
        vb = v_ref[:, vs].astype(BF16)
        hi, lo = _split(lg)
        b = jnp.concatenate([_dot(ltri2, jnp.concatenate([hi[sl], lo[sl]], axis=0)) for sl in cs], axis=0)
        b_last = [b[(ci + 1) * C - 1:(ci + 1) * C, :] for ci in range(nC)]
        b_end = jnp.concatenate([jnp.broadcast_to(bl, (C, bl.shape[1])) for bl in b_last], axis=0)
        a = (q * jnp.exp(b)).astype(BF16)
        kd = (k * jnp.exp(-b)).astype(BF16)
        k2 = (k * jnp.exp(b_end - b)).astype(BF16)
        scs = [jnp.where(causal, _dot_nt(a[sl], kd[sl]), 0.0).astype(BF16) for sl in cs]
        o_intra = jnp.concatenate([_dot(sc, vb[sl]) for sc, sl in zip(scs, cs)], axis=0)
        upd = [_dot_tn(vb[sl], k2[sl]) for sl in cs]
        st = st_scr[hh]
        o_inter = []
        for ci in range(nC):
            o_inter.append(_dot_nt(a[cs[ci]], st.astype(BF16)))
            st = st * jnp.exp(b_last[ci]) + upd[ci]
        st_scr[hh] = st
        o = o_intra + jnp.concatenate(o_inter, axis=0)
        ms = jnp.mean(o * o, axis=-1, keepdims=True)
        o_ref[:, vs] = (o * lax.rsqrt(ms + RMS_EPS) * g_ref[:, vs] * _silu(r_ref[:, vs])).astype(BF16)

    @pl.when(c == pl.num_programs(2) - 1)
    def _():
        sout_ref[...] = st_scr[...]


def _gla(gq, gk, lg, gv, gr, s0t, g, B, L, C, nC):
    GH, M, DK = gq.shape
    DV = gv.shape[1] // GH
    HPS = GLA_HEADS_PER_STEP if GH % GLA_HEADS_PER_STEP == 0 else 1
    blk = C * nC
    nblk = L // blk
    t = jnp.arange(C)
    causal = t[:, None] >= t[None, :]
    ltri2 = jnp.concatenate([causal, causal], axis=1).astype(BF16)
    cmask = causal.astype(F32)
    hm = pl.BlockSpec((HPS, blk, DK), lambda b, h, c: (h, b * nblk + c, 0))
    tokm = pl.BlockSpec((blk, HPS * DV), lambda b, h, c: (b * nblk + c, h))
    state = pl.BlockSpec((None, HPS, DV, DK), lambda b, h, c: (b, h, 0, 0))
    const = lambda a: pl.BlockSpec(a.shape, lambda b, h, c: (0,) * a.ndim)
    return pl.pallas_call(
        functools.partial(_gla_kernel, C=C, nC=nC, HPS=HPS),
        out_shape=(jax.ShapeDtypeStruct((M, GH * DV), BF16), jax.ShapeDtypeStruct((B, GH, DV, DK), F32)),
        grid=(B, GH // HPS, nblk),
        in_specs=[hm, hm, hm, tokm, tokm, state,
                  pl.BlockSpec((1, HPS * DV), lambda b, h, c: (0, h)), const(ltri2), const(cmask)],
        out_specs=(tokm, state),
        scratch_shapes=[pltpu.VMEM((HPS, DV, DK), F32)],
        compiler_params=_cparams(("parallel", "parallel", "arbitrary")),
        name="gla",
    )(gq, gk, lg, gv, gr, s0t, g, ltri2, cmask)


def _merge_ln_kernel(a_ref, b_ref, x_ref, wa_ref, wb_ref, g_ref, beta_ref, o_ref, *, alpha):
    y = _dot(a_ref[...], wa_ref[...]) + _dot(b_ref[...], wb_ref[...])
    o_ref[...] = _layer_norm(alpha * x_ref[...] + y, g_ref[...], beta_ref[...])


def _merge_ln(a, b, x, wa, wb, g, beta, alpha):
    M, D = x.shape
    tm = _tile(M, ROW_TILE)
    row = lambda n: pl.BlockSpec((tm, n), lambda i: (i, 0))
    full = lambda t: pl.BlockSpec(t.shape, lambda i: (0,) * t.ndim)
    return pl.pallas_call(
        functools.partial(_merge_ln_kernel, alpha=alpha),
        out_shape=jax.ShapeDtypeStruct((M, D), F32),
        grid=(M // tm,),
        in_specs=[row(a.shape[1]), row(b.shape[1]), row(D), full(wa), full(wb), full(g), full(beta)],
        out_specs=row(D),
        compiler_params=_cparams(("parallel",)),
        name="merge_ln",
    )(a, b, x, wa, wb, g, beta)


def _ffn_kernel(te_ref, x_ref, wg_ref, wu_ref, wd_ref, *rest, nf, tf, n_tiles, alpha, fuse_ln):
    if fuse_ln:
        g_ref, beta_ref, o_ref, *scratch = rest
    else:
        o_ref, *scratch = rest
    if nf > 1:
        xb_scr, acc_scr = scratch
    i = pl.program_id(0)

    @pl.when(i < te_ref[n_tiles])
    def _():
        xb = x_ref[...].astype(BF16)
        if nf == 1:
            y = _dot((_silu(_dot(xb, wg_ref[...])) * _dot(xb, wu_ref[...])).astype(BF16), wd_ref[...])
        else:
            xb_scr[...] = xb
            acc_scr[...] = jnp.zeros_like(acc_scr)

            def body(f, carry):
                xc = xb_scr[...]
                cols = pl.ds(pl.multiple_of(f * tf, tf), tf)
                hg = _dot(xc, wg_ref[:, cols])
                hu = _dot(xc, wu_ref[:, cols])
                acc_scr[...] += _dot((_silu(hg) * hu).astype(BF16), wd_ref[cols, :])
                return carry

            lax.fori_loop(0, nf, body, 0)
            y = acc_scr[...]
        if fuse_ln:
            o_ref[...] = _layer_norm(alpha * x_ref[...] + y, g_ref[...], beta_ref[...])
        else:
            o_ref[...] = y

    @pl.when(i >= te_ref[n_tiles])
    def _():
        o_ref[...] = jnp.zeros_like(o_ref)


def _ffn(x, te, wg, wu, wd, tm, ln=None, alpha=1.0):
    R, D = x.shape
    F = wg.shape[2]
    tf = _tile(F, FF_CHUNK)
    nf = F // tf
    n_tiles = R // tm
    row = pl.BlockSpec((tm, D), lambda i, te: (i, 0))
    w_in = pl.BlockSpec((None, D, F), lambda i, te: (te[i], 0, 0))
    w_out = pl.BlockSpec((None, F, D), lambda i, te: (te[i], 0, 0))
    vec = pl.BlockSpec((1, D), lambda i, te: (0, 0))
    in_specs = [row, w_in, w_in, w_out] + ([vec, vec] if ln else [])
    grid_spec = pltpu.PrefetchScalarGridSpec(
        num_scalar_prefetch=1, grid=(n_tiles,), in_specs=in_specs, out_specs=row,
        scratch_shapes=[pltpu.VMEM((tm, D), BF16), pltpu.VMEM((tm, D), F32)] if nf > 1 else [])
    return pl.pallas_call(
        functools.partial(_ffn_kernel, nf=nf, tf=tf, n_tiles=n_tiles, alpha=alpha, fuse_ln=bool(ln)),
        out_shape=jax.ShapeDtypeStruct((R, D), F32),
        grid_spec=grid_spec,
        compiler_params=_cparams(("arbitrary",)),
        name="ffn_ln" if ln else "ffn_experts",
    )(te, x, wg, wu, wd, *(ln or ()))


def _router_kernel(x_ref, wh_ref, wl_ref, b_ref, c0_ref, lstrict_ref, slab_ref, cnt_ref, carry_scr, *, E):
    i = pl.program_id(0)

    @pl.when(i == 0)
    def _():
        carry_scr[...] = c0_ref[...]

    xh, xl = _split(x_ref[...])
    wh, wl = wh_ref[...], wl_ref[...]
    logits = _dot(xh, wh) + (_dot(xl, wh) + _dot(xh, wl)) + b_ref[...]
    tm, W = logits.shape
    lane = lax.broadcasted_iota(jnp.int32, (tm, W), 1).astype(F32)
    neg = -jnp.inf
    lg1 = jnp.where(lane < E, logits, neg)
    m1 = jnp.max(lg1, axis=-1, keepdims=True)
    i1 = jnp.min(jnp.where(lg1 == m1, lane, float(W)), axis=-1, keepdims=True)
    lg2 = jnp.where(lane == i1, neg, lg1)
    m2 = jnp.max(lg2, axis=-1, keepdims=True)
    i2 = jnp.min(jnp.where(lg2 == m2, lane, float(W)), axis=-1, keepdims=True)
    e2 = jnp.exp(m2 - m1)
    g1 = 1.0 / (1.0 + e2)
    g2 = e2 / (1.0 + e2)
    oh = jnp.where((lane == i1) | (lane == i2), 1.0, 0.0)
    before = _dot(lstrict_ref[...], oh.astype(BF16)) + carry_scr[...]
    r1 = jnp.sum(jnp.where(lane == i1, before, 0.0), axis=-1, keepdims=True)
    r2 = jnp.sum(jnp.where(lane == i2, before, 0.0), axis=-1, keepdims=True)
    carry_scr[...] += jnp.sum(oh, axis=0, keepdims=True)
    slab = jnp.zeros((tm, W), F32)
    for col, val in enumerate((i1, i2, g1, g2, r1, r2)):
        slab = jnp.where(lane == col, val, slab)
    slab_ref[...] = slab
    cnt_ref[...] = carry_scr[...]


def _router(x, wh, wl, b, c0, E):
    M, D = x.shape
    tm = _tile(M, ROW_TILE)
    W = wh.shape[1]
    lstrict = (jnp.arange(tm)[:, None] > jnp.arange(tm)[None, :]).astype(BF16)
    full = lambda t: pl.BlockSpec(t.shape, lambda i: (0,) * t.ndim)
    return pl.pallas_call(
        functools.partial(_router_kernel, E=E),
        out_shape=(jax.ShapeDtypeStruct((M, W), F32), jax.ShapeDtypeStruct((1, W), F32)),
        grid=(M // tm,),
        in_specs=[pl.BlockSpec((tm, D), lambda i: (i, 0)), full(wh), full(wl), full(b), full(c0), full(lstrict)],
        out_specs=(pl.BlockSpec((tm, W), lambda i: (i, 0)), pl.BlockSpec((1, W), lambda i: (0, 0))),
        scratch_shapes=[pltpu.VMEM((1, W), F32)],
        compiler_params=_cparams(("arbitrary",)),
        name="router",
    )(x, wh, wl, b, c0, lstrict)


def _row_copy(src, s, dst, d, sem):
    return pltpu.make_async_copy(src.at[pl.ds(s, 1)], dst.at[pl.ds(d, 1)], sem)


def _dispatch_kernel(dest_ref, x_ref, xs_in, xs_out, sem, *, tm):
    del xs_in

    def start(r, carry):
        for j in range(TOP_K):
            _row_copy(x_ref, r, xs_out, dest_ref[0, 0, TOP_K * r + j], sem).start()
        return carry

    lax.fori_loop(0, tm, start, 0, unroll=ROW_DMA_UNROLL)
    for j in range(TOP_K):
        pltpu.make_async_copy(x_ref, xs_out.at[pl.ds(0, tm)], sem).wait()


def _dispatch(x, dest3, xs):
    M, D = x.shape
    tm = dest3.shape[2] // TOP_K
    return pl.pallas_call(
        functools.partial(_dispatch_kernel, tm=tm),
        out_shape=jax.ShapeDtypeStruct(xs.shape, xs.dtype),
        grid=(M // tm,),
        in_specs=[pl.BlockSpec((1, 1, TOP_K * tm), lambda i: (i, 0, 0), memory_space=pltpu.SMEM),
                  pl.BlockSpec((tm, D), lambda i: (i, 0)), pl.BlockSpec(memory_space=pl.ANY)],
        out_specs=pl.BlockSpec(memory_space=pl.ANY),
        scratch_shapes=[pltpu.SemaphoreType.DMA],
        input_output_aliases={2: 0},
        compiler_params=_cparams(("arbitrary",)),
        name="moe_dispatch",
    )(dest3, x, xs)


def _combine_kernel(dest_ref, slab_ref, x_ref, g_ref, beta_ref, ys_hbm, o_ref, buf, sem, *, tm, alpha):
    def start(r, carry):
        for j in range(TOP_K):
            _row_copy(ys_hbm, dest_ref[0, 0, TOP_K * r + j], buf.at[j], r, sem).start()
        return carry

    lax.fori_loop(0, tm, start, 0, unroll=ROW_DMA_UNROLL)
    for j in range(TOP_K):
        pltpu.make_async_copy(ys_hbm.at[pl.ds(0, tm)], buf.at[j], sem).wait()
    slab = slab_ref[...]
    y = slab[:, 2:3] * buf[0] + slab[:, 3:4] * buf[1]
    o_ref[...] = _layer_norm(alpha * x_ref[...] + y, g_ref[...], beta_ref[...])


def _combine(x, slab, dest3, ys, g, beta, alpha):
    M, D = x.shape
    tm = dest3.shape[2] // TOP_K
    row = lambda n: pl.BlockSpec((tm, n), lambda i: (i, 0))
    vec = pl.BlockSpec((1, D), lambda i: (0, 0))
    return pl.pallas_call(
        functools.partial(_combine_kernel, tm=tm, alpha=alpha),
        out_shape=jax.ShapeDtypeStruct((M, D), F32),
        grid=(M // tm,),
        in_specs=[pl.BlockSpec((1, 1, TOP_K * tm), lambda i: (i, 0, 0), memory_space=pltpu.SMEM),
                  row(slab.shape[1]), row(D), vec, vec, pl.BlockSpec(memory_space=pl.ANY)],
        out_specs=row(D),
        scratch_shapes=[pltpu.VMEM((TOP_K, tm, D), F32), pltpu.SemaphoreType.DMA],
        compiler_params=_cparams(("arbitrary",)),
        name="moe_combine",
    )(dest3, slab, x, g, beta, ys)


def _ffn_weights(wg, wu, wd):
    return wg.astype(BF16), wu.astype(BF16), wd.astype(BF16)


def _moe(xp, xs, router_w, router_b, wg, wu, wd, ln_g, ln_b, alpha):
    E = router_w.shape[1]
    D = xp.shape[1]
    pad = ROUTER_PAD - E
    rw = jnp.pad(router_w, ((0, 0), (0, pad)))
    rwh = rw.astype(BF16)
    rwl = (rw - rwh.astype(F32)).astype(BF16)
    rb = jnp.pad(router_b, (0, pad))[None, :]
    slab_p, cnt_p = _router(xp, rwh, rwl, rb, jnp.zeros((1, ROUTER_PAD), F32), E)
    slab_s, cnt = _router(xs, rwh, rwl, rb, cnt_p, E)

    n_assign = TOP_K * (xp.shape[0] + xs.shape[0])
    tm = ROW_TILE
    n_tiles = -(-n_assign // tm) + E
    counts = cnt[0, :E].astype(jnp.int32)
    padded = ((counts + tm - 1) // tm) * tm
    off_end = jnp.cumsum(padded)
    off_start = off_end - padded
    tile_row0 = jnp.arange(n_tiles, dtype=jnp.int32) * tm
    tile_expert = jnp.minimum(jnp.sum(tile_row0[:, None] >= off_end[None, :], axis=1), E - 1)
    te = jnp.concatenate([tile_expert, off_end[-1:] // tm]).astype(jnp.int32)

    def dests(slab):
        e = slab[:, 0:TOP_K].astype(jnp.int32)
        rank = slab[:, 4:4 + TOP_K].astype(jnp.int32)
        d = off_start[e] + rank
        t = _tile(slab.shape[0], ROW_TILE)
        return d.reshape(slab.shape[0] // t, 1, TOP_K * t)

    dest_p, dest_s = dests(slab_p), dests(slab_s)
    xsort = jnp.zeros((n_tiles * tm, D), F32)
    xsort = _dispatch(xp, dest_p, xsort)
    xsort = _dispatch(xs, dest_s, xsort)
    ys = _ffn(xsort, te, wg, wu, wd, tm)
    return (_combine(xp, slab_p, dest_p, ys, ln_g, ln_b, alpha),
            _combine(xs, slab_s, dest_s, ys, ln_g, ln_b, alpha))


def kernel(x_prompt, x_sample, cache_k, cache_v, state_gla, page_table, w_in, w_gate_up, b_gate, sb_bias,
           sb_norm_g, gla_norm_g, w_o, ln1_g, ln1_b, ln2_g, ln2_b, ffn_w_gate, ffn_w_up, ffn_w_down,
           router_w, router_b, moe_w_gate, moe_w_up, moe_w_down):
    B, L, D = x_prompt.shape
    Bd, Ld, _ = x_sample.shape
    depth, n_pool, P, H, Dh = cache_k.shape
    _, _, GH, DK, DV = state_gla.shape
    assert Dh == 64 and DK == 64, "1/sqrt(head dim) must be a power of two to fold into bf16 operands"
    SW, KW, GW = H * Dh, GH * DK, GH * DV
    n_main = 3 * SW + 2 * KW + 2 * GW
    rank = w_gate_up.shape[1]
    alpha = (2 * depth) ** 0.25
    dims = (H, Dh, GH, DK, DV)
    Ls = -(-Ld // GLA_SAMPLE_CHUNK) * GLA_SAMPLE_CHUNK

    xp = x_prompt.reshape(B * L, D)
    xs = x_sample.reshape(Bd * Ld, D)
    cache_kt = cache_k.transpose(0, 1, 3, 4, 2).reshape(depth, n_pool, SW, P)
    cache_vt = cache_v.transpose(0, 1, 3, 4, 2).reshape(depth, n_pool, SW, P)
    head_eye = jnp.eye(H, dtype=BF16)
    row2 = lambda v: v[None, :]
    outs = [[] for _ in range(6)]
    kv_prompt = (jnp.zeros((depth * B, SW, L), F32), jnp.zeros((depth * B, SW, L), F32))

    for l in range(depth):
        w_main = jnp.concatenate([w_in[l, :, :SW], w_in[l, :, 3 * SW:n_main]], axis=1).astype(BF16)
        w_kvt = w_in[l, :, SW:3 * SW].T.astype(BF16)
        w_gd = jnp.pad(w_in[l, :, n_main:], ((0, 0), (0, GLA_GATE_PAD - rank))).astype(BF16)
        w_up = jnp.pad(w_gate_up[l], ((0, GLA_GATE_PAD - rank), (0, 0))).astype(BF16)
        bg = row2(b_gate[l])
        wo = w_o[l].astype(BF16)
        g_sb_hm = sb_norm_g[l].reshape(H, 1, Dh)
        g_gla = row2(gla_norm_g[l])

        q, kt, vt, gq, gk, lg, gv, gr = _in_proj(xp, w_main, w_kvt, w_gd, w_up, bg, dims, seq_len=L,
                                                 layer=l, depth=depth, kv_prev=kv_prompt)
        kv_prompt = (kt, vt)
        a = _sb_prompt(q, kt, vt, sb_bias[l], g_sb_hm, seq0=l * B)
        C = _tile(L, GLA_CHUNK)
        nC = GLA_CHUNKS_PER_STEP
        while (L // C) % nC:
            nC //= 2
        bo, stp = _gla(gq, gk, lg, gv, gr, jnp.zeros((B, GH, DV, DK), F32), g_gla, B, L, C, nC)
        xp = _merge_ln(a, bo, xp, wo[:SW], wo[SW:], row2(ln1_g[l]), row2(ln1_b[l]), alpha)
        outs[2].append(jnp.swapaxes(stp, -1, -2))

        q, kf, vf, gq, gk, lg, gv, gr = _in_proj(xs, w_main, w_kvt, w_gd, w_up, bg, dims)
        q4 = q.reshape(H, Bd, Ld, Dh).transpose(1, 2, 0, 3)
        qbd = (q4[:, :, :, None, :] * head_eye[None, None, :, :, None]).reshape(Bd, Ld * H, SW)
        pad_new = lambda t: jnp.pad(jnp.swapaxes(t.reshape(Bd, Ld, SW), 1, 2).astype(BF16),
                                    ((0, 0), (0, 0), (0, P - Ld)))
        bias_col = jnp.tile(sb_bias[l], Ld)[:, None]
        a = _sb_decode(l, qbd, pad_new(kf), pad_new(vf), bias_col, row2(sb_norm_g[l]),
                       cache_kt, cache_vt, page_table, Ld, H, Dh)
        a = a.reshape(Bd * Ld, SW).astype(BF16)
        pad_hm = lambda t: jnp.pad(t.reshape(GH, Bd, Ld, DK), ((0, 0), (0, 0), (0, Ls - Ld), (0, 0))
                                   ).reshape(GH, Bd * Ls, DK)
        pad_tm = lambda t: jnp.pad(t.reshape(Bd, Ld, GW), ((0, 0), (0, Ls - Ld), (0, 0))).reshape(Bd * Ls, GW)
        bo, sts = _gla(pad_hm(gq), pad_hm(gk), pad_hm(lg), pad_tm(gv), pad_tm(gr),
                       jnp.swapaxes(state_gla[l], -1, -2), g_gla, Bd, Ls, Ls, 1)
        bo = bo.reshape(Bd, Ls, GW)[:, :Ld].reshape(Bd * Ld, GW)
        xs = _merge_ln(a, bo, xs, wo[:SW], wo[SW:], row2(ln1_g[l]), row2(ln1_b[l]), alpha)
        outs[3].append(kf.reshape(Bd, Ld, H, Dh))
        outs[4].append(vf.reshape(Bd, Ld, H, Dh))
        outs[5].append(jnp.swapaxes(sts, -1, -2))

        i = l // 2
        ln = (row2(ln2_g[l]), row2(ln2_b[l]))
        if l % 2 == 0:
            wg, wu, wd = _ffn_weights(ffn_w_gate[i][None], ffn_w_up[i][None], ffn_w_down[i][None])
            for_x = lambda x: _ffn(x, jnp.array([0] * (x.shape[0] // _tile(x.shape[0], ROW_TILE))
                                                + [x.shape[0] // _tile(x.shape[0], ROW_TILE)], jnp.int32),
                                   wg, wu, wd, _tile(x.shape[0], ROW_TILE), ln=ln, alpha=alpha)
            xp, xs = for_x(xp), for_x(xs)
        else:
            wg, wu, wd = _ffn_weights(moe_w_gate[i], moe_w_up[i], moe_w_down[i])
            xp, xs = _moe(xp, xs, router_w[i], router_b[i], wg, wu, wd, ln[0], ln[1], alpha)

    stack = lambda rows: jnp.stack(rows, axis=0)
    kv_out = lambda t: t.reshape(depth, B, H, Dh, L).transpose(0, 1, 4, 2, 3)
    return (xp.reshape(B, L, D), xs.reshape(Bd, Ld, D), kv_out(kv_prompt[0]), kv_out(kv_prompt[1]), stack(outs[2]),
            stack(outs[3]), stack(outs[4]), stack(outs[5]))
```

```python
import functools

import jax
import jax.numpy as jnp
from jax import lax
from jax.experimental import pallas as pl
from jax.experimental.pallas import tpu as pltpu

F32 = jnp.float32
BF16 = jnp.bfloat16

LN_EPS = 1e-5
RMS_EPS = 1e-6
GLA_GATE_NORM = 16.0
GLA_GATE_PAD = 128
ROUTER_PAD = 128
TOP_K = 2
VMEM_LIMIT = 56 * 1024 * 1024

ROW_TILE = 512
SB_TILE = 512
SB_SUB = 128
SB_HEADS_PER_ITER = 4
SB_KEY_CHUNK = 256
GLA_CHUNK = 64
GLA_CHUNKS_PER_STEP = 8
GLA_SAMPLE_CHUNK = 16
GLA_HEADS_PER_STEP = 4
GLA_MIN_CHUNK_LOG_GATE = -80.0
GLA_SLOW_ROWS = 16
FF_CHUNK = 4096
DEC_PAGES_PER_STEP = 16
ROW_DMA_UNROLL = 8


def _tile(n, pref):
    t = min(n, pref)
    while n % t:
        t -= 8
    assert t > 0 and (t % 8 == 0 or t == n)
    return t


def _cparams(sem):
    return pltpu.CompilerParams(dimension_semantics=sem, vmem_limit_bytes=VMEM_LIMIT)


def _softplus(z):
    return jnp.maximum(z, 0.0) + jnp.log(1.0 + jnp.exp(-jnp.abs(z)))


def _split(x):
    hi = x.astype(BF16)
    lo = (x - hi.astype(F32)).astype(BF16)
    return hi, lo


def _dot(a, b):
    return jnp.dot(a, b, preferred_element_type=F32)


def _dot_nt(a, b):
    return lax.dot_general(a, b, (((1,), (1,)), ((), ())), preferred_element_type=F32)


def _dot_tn(a, b):
    return lax.dot_general(a, b, (((0,), (0,)), ((), ())), preferred_element_type=F32)


def _layer_norm(x, g, b):
    mu = jnp.mean(x, axis=-1, keepdims=True)
    xc = x - mu
    var = jnp.mean(xc * xc, axis=-1, keepdims=True)
    return xc * lax.rsqrt(var + LN_EPS) * g + b


def _silu(x):
    return x / (1.0 + jnp.exp(-x))


def _in_proj_kernel(*refs, H, Dh, GH, DK, GW, kv_transposed, n_alias):
    x_ref, w_ref, wkvt_ref, wgd_ref, wup_ref, bg_ref = refs[:6]
    q_ref, k_ref, v_ref, gq_ref, gk_ref, lg_ref, gv_ref, gr_ref = refs[6 + n_alias:]
    xb = x_ref[...].astype(BF16)
    SW = H * Dh
    KW = GH * DK

    def seg(lo, n):
        return _dot(xb, w_ref[:, lo:lo + n])

    if kv_transposed:
        k_ref[...] = _dot_nt(wkvt_ref[0:SW, :], xb)
        v_ref[...] = _dot_nt(wkvt_ref[SW:2 * SW, :], xb)
    else:
        k_ref[...] = _dot_nt(xb, wkvt_ref[0:SW, :])
        v_ref[...] = _dot_nt(xb, wkvt_ref[SW:2 * SW, :])
    pq = seg(0, SW) * (Dh ** -0.5)
    for h in range(H):
        q_ref[h] = pq[:, h * Dh:(h + 1) * Dh].astype(BF16)
    o = SW
    gq = seg(o, KW) * (DK ** -0.5)
    gk = seg(o + KW, KW)
    gv_ref[...] = seg(o + 2 * KW, GW)
    gr_ref[...] = seg(o + 2 * KW + GW, GW)
    gd = _dot(xb, wgd_ref[...]).astype(BF16)
    u = _dot(gd, wup_ref[...]) + bg_ref[...]
    lg = -_softplus(-u) * (1.0 / GLA_GATE_NORM)
    for h in range(GH):
        sl = slice(h * DK, (h + 1) * DK)
        gq_ref[h] = gq[:, sl]
        gk_ref[h] = gk[:, sl]
        lg_ref[h] = lg[:, sl]


def _in_proj(x, w_main, w_kvt, w_gd, w_up, b_gate, dims, seq_len=None, layer=0, depth=1, kv_prev=()):
    H, Dh, GH, DK, DV = dims
    M, D = x.shape
    SW, KW, GW = H * Dh, GH * DK, GH * DV
    tm = _tile(seq_len or M, ROW_TILE)
    row = lambda n: pl.BlockSpec((tm, n), lambda i: (i, 0))
    hm = lambda nh, n: pl.BlockSpec((nh, tm, n), lambda i: (0, i, 0))
    full = lambda a: pl.BlockSpec(a.shape, lambda i: (0,) * a.ndim)
    if seq_len:
        per_seq = seq_len // tm
        n_seq = M // seq_len
        kv_shape = jax.ShapeDtypeStruct((depth * n_seq, SW, seq_len), F32)
        kv_spec = pl.BlockSpec((None, SW, tm), lambda i: (layer * n_seq + i // per_seq, 0, i % per_seq))
    else:
        kv_shape, kv_spec = jax.ShapeDtypeStruct((M, SW), F32), row(SW)
    out_shape = (
        jax.ShapeDtypeStruct((H, M, Dh), BF16), kv_shape, kv_shape,
        jax.ShapeDtypeStruct((GH, M, DK), F32), jax.ShapeDtypeStruct((GH, M, DK), F32),
        jax.ShapeDtypeStruct((GH, M, DK), F32),
        jax.ShapeDtypeStruct((M, GW), F32), jax.ShapeDtypeStruct((M, GW), F32),
    )
    out_specs = (hm(H, Dh), kv_spec, kv_spec, hm(GH, DK), hm(GH, DK), hm(GH, DK), row(GW), row(GW))
    return pl.pallas_call(
        functools.partial(_in_proj_kernel, H=H, Dh=Dh, GH=GH, DK=DK, GW=GW, kv_transposed=bool(seq_len),
                          n_alias=len(kv_prev)),
        out_shape=out_shape,
        grid=(M // tm,),
        in_specs=[row(D), full(w_main), full(w_kvt), full(w_gd), full(w_up), full(b_gate)]
        + [pl.BlockSpec(memory_space=pl.ANY)] * len(kv_prev),
        out_specs=out_specs,
        input_output_aliases={6 + n: 1 + n for n in range(len(kv_prev))},
        compiler_params=_cparams(("parallel",)),
        name="in_proj",
    )(x, w_main, w_kvt, w_gd, w_up, b_gate, *kv_prev)


def _sb_blocks(items, u_ref, Rs, accs):
    zs = [_dot(it["q"], it["kt"]) + it["bias"] for it in items]
    sps = [_softplus(z) if it["mask"] is None else jnp.where(it["mask"], _softplus(z), 0.0)
           for it, z in zip(items, zs)]
    parts = [_split(sp) for sp in sps]
    S = [None] * len(items)
    for w in sorted({z.shape[1] for z in zs}):
        idx = [n for n, z in enumerate(zs) if z.shape[1] == w]
        stack = jnp.concatenate([jnp.concatenate(parts[n], axis=1) for n in idx], axis=0)
        uu = jnp.concatenate([u_ref[0:w, 0:w]] * 2, axis=0)
        out = _dot(stack, uu)
        o = 0
        for n in idx:
            nr = zs[n].shape[0]
            S[n] = out[o:o + nr]
            o += nr
    Rs, accs = list(Rs), list(accs)
    for n, it in enumerate(items):
        r = it["row"]
        logw = zs[n] - S[n] - Rs[r]
        if it["mask"] is not None:
            logw = jnp.where(it["mask"], logw, -1e30)
        accs[r] = accs[r] + _dot_nt(jnp.exp(logw).astype(BF16), it["vt"])
        Rs[r] = Rs[r] + S[n][:, 0:1]
    return Rs, accs


def _sb_prompt_kernel(qi_ref, kj_ref, bias_ref, q_ref, kt_ref, vt_ref, g_ref, u_ref, o_ref, r_scr, acc_scr,
                      *, H, Dh, T, SUB, KC):
    p = pl.program_id(1)
    i, j = qi_ref[p], kj_ref[p]
    nsub = T // SUB
    HG = SB_HEADS_PER_ITER

    def run(diag):
        def head_group(hg, carry):
            rs = [slice(r * SUB, (r + 1) * SUB) for r in range(nsub)]
            items, Rs, accs = [], [], []
            for hh in range(HG):
                h = hg * HG + hh
                bias = bias_ref[h]
                hs = pl.ds(pl.multiple_of(h * Dh, Dh), Dh)
                for r in range(nsub):
                    if diag:
                        Rs.append(jnp.zeros((SUB, 1), F32))
                        accs.append(jnp.zeros((SUB, Dh), F32))
                    else:
                        Rs.append(r_scr[h, rs[r], :])
                        accs.append(acc_scr[h, rs[r], :])
                    q = q_ref[h, rs[r], :]
                    c1 = (r + 1) * SUB if diag else T
                    while c1 > 0:
                        c0 = ((c1 - 1) // KC) * KC
                        mask = None
                        if diag and c1 > r * SUB:
                            rows = lax.broadcasted_iota(jnp.int32, (SUB, c1 - c0), 0) + r * SUB
                            cols = lax.broadcasted_iota(jnp.int32, (SUB, c1 - c0), 1) + c0
                            mask = cols < rows
                        items.append(dict(row=hh * nsub + r, q=q, kt=kt_ref[hs, c0:c1].astype(BF16),
                                          vt=vt_ref[hs, c0:c1].astype(BF16), bias=bias, mask=mask))
                        c1 = c0
            Rs, accs = _sb_blocks(items, u_ref, Rs, accs)
            for hh in range(HG):
                h = hg * HG + hh
                for r in range(nsub):
                    r_scr[h, rs[r], :] = Rs[hh * nsub + r]
                    acc_scr[h, rs[r], :] = accs[hh * nsub + r]
            return carry

        lax.fori_loop(0, H // HG, head_group, 0)

    @pl.when(i == j)
    def _():
        run(True)

    @pl.when(i != j)
    def _():
        run(False)

    @pl.when(j == 0)
    def _():
        def normed(h):
            o = acc_scr[h]
            ms = jnp.mean(o * o, axis=-1, keepdims=True)
            return o * lax.rsqrt(ms + RMS_EPS) * g_ref[h]

        for hp in range(H // 2):
            o_ref[:, 2 * hp * Dh:(2 * hp + 2) * Dh] = jnp.concatenate(
                [normed(2 * hp), normed(2 * hp + 1)], axis=-1).astype(BF16)


def _sb_prompt(q_hm, kt, vt, bias, g_hm, seq0=0):
    H, M, Dh = q_hm.shape
    _, SW, L = kt.shape
    B = M // L
    T = _tile(L, SB_TILE)
    SUB = _tile(T, SB_SUB)
    KC = _tile(T, SB_KEY_CHUNK)
    nq = L // T
    assert H % SB_HEADS_PER_ITER == 0
    pairs = [(i, j) for i in range(nq) for j in range(i, -1, -1)]
    qi = jnp.array([p[0] for p in pairs], jnp.int32)
    kj = jnp.array([p[1] for p in pairs], jnp.int32)
    u = (jnp.arange(KC)[:, None] >= jnp.arange(KC)[None, :]).astype(BF16)
    kv_spec = pl.BlockSpec((None, SW, T), lambda b, p, qi, kj: (seq0 + b, 0, kj[p]))
    grid_spec = pltpu.PrefetchScalarGridSpec(
        num_scalar_prefetch=2,
        grid=(B, len(pairs)),
        in_specs=[
            pl.BlockSpec(memory_space=pltpu.SMEM),
            pl.BlockSpec((H, T, Dh), lambda b, p, qi, kj: (0, b * nq + qi[p], 0)),
            kv_spec, kv_spec,
            pl.BlockSpec((H, 1, Dh), lambda b, p, qi, kj: (0, 0, 0)),
            pl.BlockSpec((KC, KC), lambda b, p, qi, kj: (0, 0)),
        ],
        out_specs=pl.BlockSpec((T, SW), lambda b, p, qi, kj: (b * nq + qi[p], 0)),
        scratch_shapes=[pltpu.VMEM((H, T, 1), F32), pltpu.VMEM((H, T, Dh), F32)],
    )
    return pl.pallas_call(
        functools.partial(_sb_prompt_kernel, H=H, Dh=Dh, T=T, SUB=SUB, KC=KC),
        out_shape=jax.ShapeDtypeStruct((M, SW), BF16),
        grid_spec=grid_spec,
        compiler_params=_cparams(("parallel", "arbitrary")),
        name="sb_prompt",
    )(qi, kj, bias, q_hm, kt, vt, g_hm, u)


def _sb_decode_kernel(pt_ref, qbd_ref, kn_ref, vn_ref, bias_ref, g_ref, own_ref, u_ref, ck_hbm, cv_hbm,
                      o_ref, kbuf, vbuf, sem, acc_scr, r_scr, *, layer, G, Ld, H, Dh):
    b, s = pl.program_id(0), pl.program_id(1)
    nb, ns = pl.num_programs(0), pl.num_programs(1)
    NP = ns * G
    HQ = Ld * H
    P = u_ref.shape[0]
    t = b * ns + s
    slot = lax.rem(t, 2)

    def page_copies(bb, ss, sl):
        cps = []
        for i in range(G):
            page = pt_ref[bb, NP - 1 - (ss * G + i)]
            cps.append(pltpu.make_async_copy(ck_hbm.at[layer, page], kbuf.at[sl, i], sem.at[sl]))
            cps.append(pltpu.make_async_copy(cv_hbm.at[layer, page], vbuf.at[sl, i], sem.at[sl]))
        return cps

    @pl.when(t == 0)
    def _():
        for cp in page_copies(b, s, slot):
            cp.start()

    @pl.when(t + 1 < nb * ns)
    def _():
        wrap = s + 1 == ns
        for cp in page_copies(jnp.where(wrap, b + 1, b), jnp.where(wrap, 0, s + 1), 1 - slot):
            cp.start()

    for cp in page_copies(b, s, slot):
        cp.wait()

    q = qbd_ref[...]
    bias = bias_ref[...]

    @pl.when(s == 0)
    def _():
        rows = lax.broadcasted_iota(jnp.int32, (HQ, P), 0)
        cols = lax.broadcasted_iota(jnp.int32, (HQ, P), 1)
        mask = rows >= cols * H + H
        Rs, accs = _sb_blocks([dict(row=0, q=q, kt=kn_ref[...], vt=vn_ref[...], bias=bias, mask=mask)], u_ref,
                              [jnp.zeros((HQ, 1), F32)], [jnp.zeros((HQ, H * Dh), F32)])
        r_scr[...] = Rs[0]
        acc_scr[...] = accs[0]

    items = [dict(row=0, q=q, kt=kbuf[slot, i].astype(BF16), vt=vbuf[slot, i].astype(BF16), bias=bias, mask=None)
             for i in range(G)]
    Rs, accs = _sb_blocks(items, u_ref, [r_scr[...]], [acc_scr[...]])
    r_scr[...] = Rs[0]
    acc_scr[...] = accs[0]

    @pl.when(s == pl.num_programs(1) - 1)
    def _():
        o = acc_scr[...] * own_ref[...]
        ms = jnp.sum(o * o, axis=-1, keepdims=True) * (1.0 / Dh)
        o = o * lax.rsqrt(ms + RMS_EPS)
        g = g_ref[...]
        for t in range(Ld):
            o_ref[t:t + 1, :] = jnp.sum(o[t * H:(t + 1) * H, :], axis=0, keepdims=True) * g


def _sb_decode(layer, qbd, kt_new, vt_new, bias_col, g, cache_kt, cache_vt, page_table, Ld, H, Dh):
    Bd, HQ, SW = qbd.shape
    NP = page_table.shape[1]
    P = cache_kt.shape[3]
    G = DEC_PAGES_PER_STEP
    while NP % G:
        G //= 2
    u = (jnp.arange(P)[:, None] >= jnp.arange(P)[None, :]).astype(BF16)
    own = (jnp.arange(HQ)[:, None] % H == jnp.arange(SW)[None, :] // Dh).astype(F32)

    per_b = lambda r, c: pl.BlockSpec((None, r, c), lambda b, s, pt: (b, 0, 0))
    const = lambda a: pl.BlockSpec(a.shape, lambda b, s, pt: (0,) * a.ndim)
    hbm = pl.BlockSpec(memory_space=pl.ANY)
    grid_spec = pltpu.PrefetchScalarGridSpec(
        num_scalar_prefetch=1,
        grid=(Bd, NP // G),
        in_specs=[per_b(HQ, SW), per_b(SW, P), per_b(SW, P), const(bias_col), const(g), const(own), const(u),
                  hbm, hbm],
        out_specs=per_b(Ld, SW),
        scratch_shapes=[pltpu.VMEM((2, G, SW, P), F32), pltpu.VMEM((2, G, SW, P), F32),
                        pltpu.SemaphoreType.DMA((2,)),
                        pltpu.VMEM((HQ, SW), F32), pltpu.VMEM((HQ, 1), F32)],
    )
    return pl.pallas_call(
        functools.partial(_sb_decode_kernel, layer=layer, G=G, Ld=Ld, H=H, Dh=Dh),
        out_shape=jax.ShapeDtypeStruct((Bd, Ld, SW), F32),
        grid_spec=grid_spec,
        compiler_params=_cparams(("arbitrary", "arbitrary")),
        name="sb_decode",
    )(page_table, qbd, kt_new, vt_new, bias_col, g, own, u, cache_kt, cache_vt)


def _gla_kernel(q_ref, k_ref, lg_ref, v_ref, r_ref, s0_ref, g_ref, ltri2_ref, cmask_ref, o_ref, sout_ref, st_scr,
                *, C, nC, HPS):
    c = pl.program_id(2)

    @pl.when(c == 0)
    def _():
        st_scr[...] = s0_ref[...]

    cs = [slice(ci * C, (ci + 1) * C) for ci in range(nC)]
    ltri2 = ltri2_ref[...]
    DV = st_scr.shape[1]
    R = C * nC

    def epilogue(o, rows, vs):
        ms = jnp.mean(o * o, axis=-1, keepdims=True)
        o_ref[rows, vs] = (o * lax.rsqrt(ms + RMS_EPS) * g_ref[:, vs] * _silu(r_ref[rows, vs])).astype(BF16)

    bs = []
    for hh in range(HPS):
        hi, lo = _split(lg_ref[hh])
        bs.append(jnp.concatenate([_dot(ltri2, jnp.concatenate([hi[sl], lo[sl]], axis=0)) for sl in cs], axis=0))
    lowest = bs[0]
    for b in bs[1:]:
        lowest = jnp.minimum(lowest, b)
    safe = jnp.min(lowest) > GLA_MIN_CHUNK_LOG_GATE

    @pl.when(safe)
    def _():
        causal = cmask_ref[...] > 0.5
        for hh in range(HPS):
            vs = slice(hh * DV, (hh + 1) * DV)
            q, k, b = q_ref[hh], k_ref[hh], bs[hh]
            vb = v_ref[:, vs].astype(BF16)
            b_last = [b[(ci + 1) * C - 1:(ci + 1) * C, :] for ci in range(nC)]
            b_end = jnp.concatenate([jnp.broadcast_to(bl, (C, bl.shape[1])) for bl in b_last], axis=0)
            a = (q * jnp.exp(b)).astype(BF16)
            kd = (k * jnp.exp(-b)).astype(BF16)
            k2 = (k * jnp.exp(b_end - b)).astype(BF16)
            scs = [jnp.where(causal, _dot_nt(a[sl], kd[sl]), 0.0).astype(BF16) for sl in cs]
            o_intra = jnp.concatenate([_dot(sc, vb[sl]) for sc, sl in zip(scs, cs)], axis=0)
            upd = [_dot_tn(vb[sl], k2[sl]) for sl in cs]
            st = st_scr[hh]
            o_inter = []
            for ci in range(nC):
                o_inter.append(_dot_nt(a[cs[ci]], st.astype(BF16)))
                st = st * jnp.exp(b_last[ci]) + upd[ci]
            st_scr[hh] = st
            epilogue(o_intra + jnp.concatenate(o_inter, axis=0), slice(0, R), vs)

    @pl.when(jnp.logical_not(safe))
    def _():
        W = GLA_SLOW_ROWS
        rid = lax.broadcasted_iota(jnp.int32, (W, 1), 0)
        for hh in range(HPS):
            vs = slice(hh * DV, (hh + 1) * DV)

            def slab(t, st):
                rows = pl.ds(pl.multiple_of(t * W, W), W)
                q8, k8, g8 = q_ref[hh, rows, :], k_ref[hh, rows, :], lg_ref[hh, rows, :]
                v8 = v_ref[rows, vs]
                o8 = jnp.zeros((W, DV), F32)
                for j in range(W):
                    sel = rid == j
                    gj = jnp.sum(jnp.where(sel, g8, 0.0), axis=0, keepdims=True)
                    kj = jnp.where(sel, k8, 0.0).astype(BF16)
                    vj = jnp.where(sel, v8, 0.0).astype(BF16)
                    qj = jnp.where(sel, q8, 0.0).astype(BF16)
                    st = st * jnp.exp(gj) + _dot_tn(vj, kj)
                    o8 = o8 + _dot_nt(qj, st.astype(BF16))
                epilogue(o8, rows, vs)
                return st

            st_scr[hh] = lax.fori_loop(0, R // W, slab, st_scr[hh])

    @pl.when(c == pl.num_programs(2) - 1)
    def _():
        sout_ref[...] = st_scr[...]


def _gla(gq, gk, lg, gv, gr, s0t, g, B, L, C, nC):
    GH, M, DK = gq.shape
    DV = gv.shape[1] // GH
    HPS = GLA_HEADS_PER_STEP if GH % GLA_HEADS_PER_STEP == 0 else 1
    blk = C * nC
    nblk = L // blk
    t = jnp.arange(C)
    causal = t[:, None] >= t[None, :]
    ltri2 = jnp.concatenate([causal, causal], axis=1).astype(BF16)
    cmask = causal.astype(F32)
    hm = pl.BlockSpec((HPS, blk, DK), lambda b, h, c: (h, b * nblk + c, 0))
    tokm = pl.BlockSpec((blk, HPS * DV), lambda b, h, c: (b * nblk + c, h))
    state = pl.BlockSpec((None, HPS, DV, DK), lambda b, h, c: (b, h, 0, 0))
    const = lambda a: pl.BlockSpec(a.shape, lambda b, h, c: (0,) * a.ndim)
    return pl.pallas_call(
        functools.partial(_gla_kernel, C=C, nC=nC, HPS=HPS),
        out_shape=(jax.ShapeDtypeStruct((M, GH * DV), BF16), jax.ShapeDtypeStruct((B, GH, DV, DK), F32)),
        grid=(B, GH // HPS, nblk),
        in_specs=[hm, hm, hm, tokm, tokm, state,
                  pl.BlockSpec((1, HPS * DV), lambda b, h, c: (0, h)), const(ltri2), const(cmask)],
        out_specs=(tokm, state),
        scratch_shapes=[pltpu.VMEM((HPS, DV, DK), F32)],
        compiler_params=_cparams(("parallel", "parallel", "arbitrary")),
        name="gla",
    )(gq, gk, lg, gv, gr, s0t, g, ltri2, cmask)


def _merge_ln_kernel(a_ref, b_ref, x_ref, wa_ref, wb_ref, g_ref, beta_ref, o_ref, *, alpha):
    y = _dot(a_ref[...], wa_ref[...]) + _dot(b_ref[...], wb_ref[...])
    o_ref[...] = _layer_norm(alpha * x_ref[...] + y, g_ref[...], beta_ref[...])


def _merge_ln(a, b, x, wa, wb, g, beta, alpha):
    M, D = x.shape
    tm = _tile(M, ROW_TILE)
    row = lambda n: pl.BlockSpec((tm, n), lambda i: (i, 0))
    full = lambda t: pl.BlockSpec(t.shape, lambda i: (0,) * t.ndim)
    return pl.pallas_call(
        functools.partial(_merge_ln_kernel, alpha=alpha),
        out_shape=jax.ShapeDtypeStruct((M, D), F32),
        grid=(M // tm,),
        in_specs=[row(a.shape[1]), row(b.shape[1]), row(D), full(wa), full(wb), full(g), full(beta)],
        out_specs=row(D),
        compiler_params=_cparams(("parallel",)),
        name="merge_ln",
    )(a, b, x, wa, wb, g, beta)


def _ffn_kernel(te_ref, x_ref, wg_ref, wu_ref, wd_ref, *rest, nf, tf, n_tiles, alpha, fuse_ln):
    if fuse_ln:
        g_ref, beta_ref, o_ref, *scratch = rest
    else:
        o_ref, *scratch = rest
    if nf > 1:
        xb_scr, acc_scr = scratch
    i = pl.program_id(0)

    @pl.when(i < te_ref[n_tiles])
    def _():
        xb = x_ref[...].astype(BF16)
        if nf == 1:
            y = _dot((_silu(_dot(xb, wg_ref[...])) * _dot(xb, wu_ref[...])).astype(BF16), wd_ref[...])
        else:
            xb_scr[...] = xb
            acc_scr[...] = jnp.zeros_like(acc_scr)

            def body(f, carry):
                xc = xb_scr[...]
                cols = pl.ds(pl.multiple_of(f * tf, tf), tf)
                hg = _dot(xc, wg_ref[:, cols])
                hu = _dot(xc, wu_ref[:, cols])
                acc_scr[...] += _dot((_silu(hg) * hu).astype(BF16), wd_ref[cols, :])
                return carry

            lax.fori_loop(0, nf, body, 0)
            y = acc_scr[...]
        if fuse_ln:
            o_ref[...] = _layer_norm(alpha * x_ref[...] + y, g_ref[...], beta_ref[...])
        else:
            o_ref[...] = y

    @pl.when(i >= te_ref[n_tiles])
    def _():
        o_ref[...] = jnp.zeros_like(o_ref)


def _ffn(x, te, wg, wu, wd, tm, ln=None, alpha=1.0):
    R, D = x.shape
    F = wg.shape[2]
    tf = _tile(F, FF_CHUNK)
    nf = F // tf
    n_tiles = R // tm
    row = pl.BlockSpec((tm, D), lambda i, te: (i, 0))
    w_in = pl.BlockSpec((None, D, F), lambda i, te: (te[i], 0, 0))
    w_out = pl.BlockSpec((None, F, D), lambda i, te: (te[i], 0, 0))
    vec = pl.BlockSpec((1, D), lambda i, te: (0, 0))
    in_specs = [row, w_in, w_in, w_out] + ([vec, vec] if ln else [])
    grid_spec = pltpu.PrefetchScalarGridSpec(
        num_scalar_prefetch=1, grid=(n_tiles,), in_specs=in_specs, out_specs=row,
        scratch_shapes=[pltpu.VMEM((tm, D), BF16), pltpu.VMEM((tm, D), F32)] if nf > 1 else [])
    return pl.pallas_call(
        functools.partial(_ffn_kernel, nf=nf, tf=tf, n_tiles=n_tiles, alpha=alpha, fuse_ln=bool(ln)),
        out_shape=jax.ShapeDtypeStruct((R, D), F32),
        grid_spec=grid_spec,
        compiler_params=_cparams(("arbitrary",)),
        name="ffn_ln" if ln else "ffn_experts",
    )(te, x, wg, wu, wd, *(ln or ()))


def _router_kernel(x_ref, wh_ref, wl_ref, b_ref, c0_ref, lstrict_ref, slab_ref, cnt_ref, carry_scr, *, E):
    i = pl.program_id(0)

    @pl.when(i == 0)
    def _():
        carry_scr[...] = c0_ref[...]

    xh, xl = _split(x_ref[...])
    wh, wl = wh_ref[...], wl_ref[...]
    logits = _dot(xh, wh) + (_dot(xl, wh) + _dot(xh, wl)) + b_ref[...]
    tm, W = logits.shape
    lane = lax.broadcasted_iota(jnp.int32, (tm, W), 1).astype(F32)
    neg = -jnp.inf
    lg1 = jnp.where(lane < E, logits, neg)
    m1 = jnp.max(lg1, axis=-1, keepdims=True)
    i1 = jnp.min(jnp.where(lg1 == m1, lane, float(W)), axis=-1, keepdims=True)
    lg2 = jnp.where(lane == i1, neg, lg1)
    m2 = jnp.max(lg2, axis=-1, keepdims=True)
    i2 = jnp.min(jnp.where(lg2 == m2, lane, float(W)), axis=-1, keepdims=True)
    e2 = jnp.exp(m2 - m1)
    g1 = 1.0 / (1.0 + e2)
    g2 = e2 / (1.0 + e2)
    oh = jnp.where((lane == i1) | (lane == i2), 1.0, 0.0)
    before = _dot(lstrict_ref[...], oh.astype(BF16)) + carry_scr[...]
    r1 = jnp.sum(jnp.where(lane == i1, before, 0.0), axis=-1, keepdims=True)
    r2 = jnp.sum(jnp.where(lane == i2, before, 0.0), axis=-1, keepdims=True)
    carry_scr[...] += jnp.sum(oh, axis=0, keepdims=True)
    slab = jnp.zeros((tm, W), F32)
    for col, val in enumerate((i1, i2, g1, g2, r1, r2)):
        slab = jnp.where(lane == col, val, slab)
    slab_ref[...] = slab
    cnt_ref[...] = carry_scr[...]


def _router(x, wh, wl, b, c0, E):
    M, D = x.shape
    tm = _tile(M, ROW_TILE)
    W = wh.shape[1]
    lstrict = (jnp.arange(tm)[:, None] > jnp.arange(tm)[None, :]).astype(BF16)
    full = lambda t: pl.BlockSpec(t.shape, lambda i: (0,) * t.ndim)
    return pl.pallas_call(
        functools.partial(_router_kernel, E=E),
        out_shape=(jax.ShapeDtypeStruct((M, W), F32), jax.ShapeDtypeStruct((1, W), F32)),
        grid=(M // tm,),
        in_specs=[pl.BlockSpec((tm, D), lambda i: (i, 0)), full(wh), full(wl), full(b), full(c0), full(lstrict)],
        out_specs=(pl.BlockSpec((tm, W), lambda i: (i, 0)), pl.BlockSpec((1, W), lambda i: (0, 0))),
        scratch_shapes=[pltpu.VMEM((1, W), F32)],
        compiler_params=_cparams(("arbitrary",)),
        name="router",
    )(x, wh, wl, b, c0, lstrict)


def _row_copy(src, s, dst, d, sem):
    return pltpu.make_async_copy(src.at[pl.ds(s, 1)], dst.at[pl.ds(d, 1)], sem)


def _dispatch_kernel(dest_ref, x_ref, xs_in, xs_out, sem, *, tm):
    del xs_in

    def start(r, carry):
        for j in range(TOP_K):
            _row_copy(x_ref, r, xs_out, dest_ref[0, 0, TOP_K * r + j], sem).start()
        return carry

    lax.fori_loop(0, tm, start, 0, unroll=ROW_DMA_UNROLL)
    for j in range(TOP_K):
        pltpu.make_async_copy(x_ref, xs_out.at[pl.ds(0, tm)], sem).wait()


def _dispatch(x, dest3, xs):
    M, D = x.shape
    tm = dest3.shape[2] // TOP_K
    return pl.pallas_call(
        functools.partial(_dispatch_kernel, tm=tm),
        out_shape=jax.ShapeDtypeStruct(xs.shape, xs.dtype),
        grid=(M // tm,),
        in_specs=[pl.BlockSpec((1, 1, TOP_K * tm), lambda i: (i, 0, 0), memory_space=pltpu.SMEM),
                  pl.BlockSpec((tm, D), lambda i: (i, 0)), pl.BlockSpec(memory_space=pl.ANY)],
        out_specs=pl.BlockSpec(memory_space=pl.ANY),
        scratch_shapes=[pltpu.SemaphoreType.DMA],
        input_output_aliases={2: 0},
        compiler_params=_cparams(("arbitrary",)),
        name="moe_dispatch",
    )(dest3, x, xs)


def _combine_kernel(dest_ref, slab_ref, x_ref, g_ref, beta_ref, ys_hbm, o_ref, buf, sem, *, tm, alpha):
    def start(r, carry):
        for j in range(TOP_K):
            _row_copy(ys_hbm, dest_ref[0, 0, TOP_K * r + j], buf.at[j], r, sem).start()
        return carry

    lax.fori_loop(0, tm, start, 0, unroll=ROW_DMA_UNROLL)
    for j in range(TOP_K):
        pltpu.make_async_copy(ys_hbm.at[pl.ds(0, tm)], buf.at[j], sem).wait()
    slab = slab_ref[...]
    y = slab[:, 2:3] * buf[0] + slab[:, 3:4] * buf[1]
    o_ref[...] = _layer_norm(alpha * x_ref[...] + y, g_ref[...], beta_ref[...])


def _combine(x, slab, dest3, ys, g, beta, alpha):
    M, D = x.shape
    tm = dest3.shape[2] // TOP_K
    row = lambda n: pl.BlockSpec((tm, n), lambda i: (i, 0))
    vec = pl.BlockSpec((1, D), lambda i: (0, 0))
    return pl.pallas_call(
        functools.partial(_combine_kernel, tm=tm, alpha=alpha),
        out_shape=jax.ShapeDtypeStruct((M, D), F32),
        grid=(M // tm,),
        in_specs=[pl.BlockSpec((1, 1, TOP_K * tm), lambda i: (i, 0, 0), memory_space=pltpu.SMEM),
                  row(slab.shape[1]), row(D), vec, vec, pl.BlockSpec(memory_space=pl.ANY)],
        out_specs=row(D),
        scratch_shapes=[pltpu.VMEM((TOP_K, tm, D), F32), pltpu.SemaphoreType.DMA],
        compiler_params=_cparams(("arbitrary",)),
        name="moe_combine",
    )(dest3, slab, x, g, beta, ys)


def _ffn_weights(wg, wu, wd):
    return wg.astype(BF16), wu.astype(BF16), wd.astype(BF16)


def _moe(xp, xs, router_w, router_b, wg, wu, wd, ln_g, ln_b, alpha):
    E = router_w.shape[1]
    D = xp.shape[1]
    pad = ROUTER_PAD - E
    rw = jnp.pad(router_w, ((0, 0), (0, pad)))
    rwh = rw.astype(BF16)
    rwl = (rw - rwh.astype(F32)).astype(BF16)
    rb = jnp.pad(router_b, (0, pad))[None, :]
    slab_p, cnt_p = _router(xp, rwh, rwl, rb, jnp.zeros((1, ROUTER_PAD), F32), E)
    slab_s, cnt = _router(xs, rwh, rwl, rb, cnt_p, E)

    n_assign = TOP_K * (xp.shape[0] + xs.shape[0])
    tm = ROW_TILE
    n_tiles = -(-n_assign // tm) + E
    counts = cnt[0, :E].astype(jnp.int32)
    padded = ((counts + tm - 1) // tm) * tm
    off_end = jnp.cumsum(padded)
    off_start = off_end - padded
    tile_row0 = jnp.arange(n_tiles, dtype=jnp.int32) * tm
    tile_expert = jnp.minimum(jnp.sum(tile_row0[:, None] >= off_end[None, :], axis=1), E - 1)
    te = jnp.concatenate([tile_expert, off_end[-1:] // tm]).astype(jnp.int32)

    def dests(slab):
        e = slab[:, 0:TOP_K].astype(jnp.int32)
        rank = slab[:, 4:4 + TOP_K].astype(jnp.int32)
        d = off_start[e] + rank
        t = _tile(slab.shape[0], ROW_TILE)
        return d.reshape(slab.shape[0] // t, 1, TOP_K * t)

    dest_p, dest_s = dests(slab_p), dests(slab_s)
    xsort = jnp.zeros((n_tiles * tm, D), F32)
    xsort = _dispatch(xp, dest_p, xsort)
    xsort = _dispatch(xs, dest_s, xsort)
    ys = _ffn(xsort, te, wg, wu, wd, tm)
    return (_combine(xp, slab_p, dest_p, ys, ln_g, ln_b, alpha),
            _combine(xs, slab_s, dest_s, ys, ln_g, ln_b, alpha))


def kernel(x_prompt, x_sample, cache_k, cache_v, state_gla, page_table, w_in, w_gate_up, b_gate, sb_bias,
           sb_norm_g, gla_norm_g, w_o, ln1_g, ln1_b, ln2_g, ln2_b, ffn_w_gate, ffn_w_up, ffn_w_down,
           router_w, router_b, moe_w_gate, moe_w_up, moe_w_down):
    B, L, D = x_prompt.shape
    Bd, Ld, _ = x_sample.shape
    depth, n_pool, P, H, Dh = cache_k.shape
    _, _, GH, DK, DV = state_gla.shape
    assert Dh == 64 and DK == 64, "1/sqrt(head dim) must be a power of two to fold into bf16 operands"
    SW, KW, GW = H * Dh, GH * DK, GH * DV
    n_main = 3 * SW + 2 * KW + 2 * GW
    rank = w_gate_up.shape[1]
    alpha = (2 * depth) ** 0.25
    dims = (H, Dh, GH, DK, DV)
    Ls = -(-Ld // GLA_SAMPLE_CHUNK) * GLA_SAMPLE_CHUNK

    xp = x_prompt.reshape(B * L, D)
    xs = x_sample.reshape(Bd * Ld, D)
    cache_kt = cache_k.transpose(0, 1, 3, 4, 2).reshape(depth, n_pool, SW, P)
    cache_vt = cache_v.transpose(0, 1, 3, 4, 2).reshape(depth, n_pool, SW, P)
    head_eye = jnp.eye(H, dtype=BF16)
    row2 = lambda v: v[None, :]
    outs = [[] for _ in range(6)]
    kv_prompt = (jnp.zeros((depth * B, SW, L), F32), jnp.zeros((depth * B, SW, L), F32))

    for l in range(depth):
        w_main = jnp.concatenate([w_in[l, :, :SW], w_in[l, :, 3 * SW:n_main]], axis=1).astype(BF16)
        w_kvt = w_in[l, :, SW:3 * SW].T.astype(BF16)
        w_gd = jnp.pad(w_in[l, :, n_main:], ((0, 0), (0, GLA_GATE_PAD - rank))).astype(BF16)
        w_up = jnp.pad(w_gate_up[l], ((0, GLA_GATE_PAD - rank), (0, 0))).astype(BF16)
        bg = row2(b_gate[l])
        wo = w_o[l].astype(BF16)
        g_sb_hm = sb_norm_g[l].reshape(H, 1, Dh)
        g_gla = row2(gla_norm_g[l])

        q, kt, vt, gq, gk, lg, gv, gr = _in_proj(xp, w_main, w_kvt, w_gd, w_up, bg, dims, seq_len=L,
                                                 layer=l, depth=depth, kv_prev=kv_prompt)
        kv_prompt = (kt, vt)
        a = _sb_prompt(q, kt, vt, sb_bias[l], g_sb_hm, seq0=l * B)
        C = _tile(L, GLA_CHUNK)
        nC = GLA_CHUNKS_PER_STEP
        while (L // C) % nC:
            nC //= 2
        bo, stp = _gla(gq, gk, lg, gv, gr, jnp.zeros((B, GH, DV, DK), F32), g_gla, B, L, C, nC)
        xp = _merge_ln(a, bo, xp, wo[:SW], wo[SW:], row2(ln1_g[l]), row2(ln1_b[l]), alpha)
        outs[2].append(jnp.swapaxes(stp, -1, -2))

        q, kf, vf, gq, gk, lg, gv, gr = _in_proj(xs, w_main, w_kvt, w_gd, w_up, bg, dims)
        q4 = q.reshape(H, Bd, Ld, Dh).transpose(1, 2, 0, 3)
        qbd = (q4[:, :, :, None, :] * head_eye[None, None, :, :, None]).reshape(Bd, Ld * H, SW)
        pad_new = lambda t: jnp.pad(jnp.swapaxes(t.reshape(Bd, Ld, SW), 1, 2).astype(BF16),
                                    ((0, 0), (0, 0), (0, P - Ld)))
        bias_col = jnp.tile(sb_bias[l], Ld)[:, None]
        a = _sb_decode(l, qbd, pad_new(kf), pad_new(vf), bias_col, row2(sb_norm_g[l]),
                       cache_kt, cache_vt, page_table, Ld, H, Dh)
        a = a.reshape(Bd * Ld, SW).astype(BF16)
        pad_hm = lambda t: jnp.pad(t.reshape(GH, Bd, Ld, DK), ((0, 0), (0, 0), (0, Ls - Ld), (0, 0))
                                   ).reshape(GH, Bd * Ls, DK)
        pad_tm = lambda t: jnp.pad(t.reshape(Bd, Ld, GW), ((0, 0), (0, Ls - Ld), (0, 0))).reshape(Bd * Ls, GW)
        bo, sts = _gla(pad_hm(gq), pad_hm(gk), pad_hm(lg), pad_tm(gv), pad_tm(gr),
                       jnp.swapaxes(state_gla[l], -1, -2), g_gla, Bd, Ls, Ls, 1)
        bo = bo.reshape(Bd, Ls, GW)[:, :Ld].reshape(Bd * Ld, GW)
        xs = _merge_ln(a, bo, xs, wo[:SW], wo[SW:], row2(ln1_g[l]), row2(ln1_b[l]), alpha)
        outs[3].append(kf.reshape(Bd, Ld, H, Dh))
        outs[4].append(vf.reshape(Bd, Ld, H, Dh))
        outs[5].append(jnp.swapaxes(sts, -1, -2))

        i = l // 2
        ln = (row2(ln2_g[l]), row2(ln2_b[l]))
        if l % 2 == 0:
            wg, wu, wd = _ffn_weights(ffn_w_gate[i][None], ffn_w_up[i][None], ffn_w_down[i][None])
            for_x = lambda x: _ffn(x, jnp.array([0] * (x.shape[0] // _tile(x.shape[0], ROW_TILE))
                                                + [x.shape[0] // _tile(x.shape[0], ROW_TILE)], jnp.int32),
                                   wg, wu, wd, _tile(x.shape[0], ROW_TILE), ln=ln, alpha=alpha)
            xp, xs = for_x(xp), for_x(xs)
        else:
            wg, wu, wd = _ffn_weights(moe_w_gate[i], moe_w_up[i], moe_w_down[i])
            xp, xs = _moe(xp, xs, router_w[i], router_b[i], wg, wu, wd, ln[0], ln[1], alpha)

    stack = lambda rows: jnp.stack(rows, axis=0)
    kv_out = lambda t: t.reshape(depth, B, H, Dh, L).transpose(0, 1, 4, 2, 3)
    return (xp.reshape(B, L, D), xs.reshape(Bd, Ld, D), kv_out(kv_prompt[0]), kv_out(kv_prompt[1]), stack(outs[2]),
            stack(outs[3]), stack(outs[4]), stack(outs[5]))
```

```python
import functools

import jax
import jax.numpy as jnp
from jax import lax
from jax.experimental import pallas as pl
from jax.experimental.pallas import tpu as pltpu

F32 = jnp.float32
BF16 = jnp.bfloat16

LN_EPS = 1e-5
RMS_EPS = 1e-6
GLA_GATE_NORM = 16.0
GLA_GATE_PAD = 128
ROUTER_PAD = 128
TOP_K = 2
VMEM_LIMIT = 56 * 1024 * 1024

ROW_TILE = 512
SB_TILE = 512
SB_SUB = 128
SB_HEADS_PER_ITER = 8
SB_KEY_CHUNK = 256
GLA_CHUNK = 64
GLA_CHUNKS_PER_STEP = 8
GLA_SAMPLE_CHUNK = 16
GLA_HEADS_PER_STEP = 4
GLA_MIN_CHUNK_LOG_GATE = -80.0
GLA_SLOW_ROWS = 16
FF_CHUNK = 4096
DEC_PAGES_PER_STEP = 16
ROW_DMA_UNROLL = 8


def _tile(n, pref):
    t = min(n, pref)
    while n % t:
        t -= 8
    assert t > 0 and (t % 8 == 0 or t == n)
    return t


def _cparams(sem):
    return pltpu.CompilerParams(dimension_semantics=sem, vmem_limit_bytes=VMEM_LIMIT)


def _softplus(z):
    return jnp.maximum(z, 0.0) + jnp.log(1.0 + jnp.exp(-jnp.abs(z)))


def _split(x):
    hi = x.astype(BF16)
    lo = (x - hi.astype(F32)).astype(BF16)
    return hi, lo


def _dot(a, b):
    return jnp.dot(a, b, preferred_element_type=F32)


def _dot_nt(a, b):
    return lax.dot_general(a, b, (((1,), (1,)), ((), ())), preferred_element_type=F32)


def _dot_tn(a, b):
    return lax.dot_general(a, b, (((0,), (0,)), ((), ())), preferred_element_type=F32)


def _layer_norm(x, g, b):
    mu = jnp.mean(x, axis=-1, keepdims=True)
    xc = x - mu
    var = jnp.mean(xc * xc, axis=-1, keepdims=True)
    return xc * lax.rsqrt(var + LN_EPS) * g + b


def _silu(x):
    return x / (1.0 + jnp.exp(-x))


def _in_proj_kernel(*refs, H, Dh, GH, DK, GW, kv_transposed, n_alias):
    x_ref, w_ref, wkvt_ref, wgd_ref, wup_ref, bg_ref = refs[:6]
    q_ref, k_ref, v_ref, gq_ref, gk_ref, lg_ref, gv_ref, gr_ref = refs[6 + n_alias:]
    xb = x_ref[...].astype(BF16)
    SW = H * Dh
    KW = GH * DK

    def seg(lo, n):
        return _dot(xb, w_ref[:, lo:lo + n])

    if kv_transposed:
        k_ref[...] = _dot_nt(wkvt_ref[0:SW, :], xb)
        v_ref[...] = _dot_nt(wkvt_ref[SW:2 * SW, :], xb)
    else:
        k_ref[...] = _dot_nt(xb, wkvt_ref[0:SW, :])
        v_ref[...] = _dot_nt(xb, wkvt_ref[SW:2 * SW, :])
    pq = seg(0, SW) * (Dh ** -0.5)
    for h in range(H):
        q_ref[h] = pq[:, h * Dh:(h + 1) * Dh].astype(BF16)
    o = SW
    gq = seg(o, KW) * (DK ** -0.5)
    gk = seg(o + KW, KW)
    gv_ref[...] = seg(o + 2 * KW, GW)
    gr_ref[...] = seg(o + 2 * KW + GW, GW)
    gd = _dot(xb, wgd_ref[...]).astype(BF16)
    u = _dot(gd, wup_ref[...]) + bg_ref[...]
    lg = -_softplus(-u) * (1.0 / GLA_GATE_NORM)
    for h in range(GH):
        sl = slice(h * DK, (h + 1) * DK)
        gq_ref[h] = gq[:, sl]
        gk_ref[h] = gk[:, sl]
        lg_ref[h] = lg[:, sl]


def _in_proj(x, w_main, w_kvt, w_gd, w_up, b_gate, dims, seq_len=None, layer=0, depth=1, kv_prev=()):
    H, Dh, GH, DK, DV = dims
    M, D = x.shape
    SW, KW, GW = H * Dh, GH * DK, GH * DV
    tm = _tile(seq_len or M, ROW_TILE)
    row = lambda n: pl.BlockSpec((tm, n), lambda i: (i, 0))
    hm = lambda nh, n: pl.BlockSpec((nh, tm, n), lambda i: (0, i, 0))
    full = lambda a: pl.BlockSpec(a.shape, lambda i: (0,) * a.ndim)
    if seq_len:
        per_seq = seq_len // tm
        n_seq = M // seq_len
        kv_shape = jax.ShapeDtypeStruct((depth * n_seq, SW, seq_len), F32)
        kv_spec = pl.BlockSpec((None, SW, tm), lambda i: (layer * n_seq + i // per_seq, 0, i % per_seq))
    else:
        kv_shape, kv_spec = jax.ShapeDtypeStruct((M, SW), F32), row(SW)
    out_shape = (
        jax.ShapeDtypeStruct((H, M, Dh), BF16), kv_shape, kv_shape,
        jax.ShapeDtypeStruct((GH, M, DK), F32), jax.ShapeDtypeStruct((GH, M, DK), F32),
        jax.ShapeDtypeStruct((GH, M, DK), F32),
        jax.ShapeDtypeStruct((M, GW), F32), jax.ShapeDtypeStruct((M, GW), F32),
    )
    out_specs = (hm(H, Dh), kv_spec, kv_spec, hm(GH, DK), hm(GH, DK), hm(GH, DK), row(GW), row(GW))
    return pl.pallas_call(
        functools.partial(_in_proj_kernel, H=H, Dh=Dh, GH=GH, DK=DK, GW=GW, kv_transposed=bool(seq_len),
                          n_alias=len(kv_prev)),
        out_shape=out_shape,
        grid=(M // tm,),
        in_specs=[row(D), full(w_main), full(w_kvt), full(w_gd), full(w_up), full(b_gate)]
        + [pl.BlockSpec(memory_space=pl.ANY)] * len(kv_prev),
        out_specs=out_specs,
        input_output_aliases={6 + n: 1 + n for n in range(len(kv_prev))},
        compiler_params=_cparams(("parallel",)),
        name="in_proj",
    )(x, w_main, w_kvt, w_gd, w_up, b_gate, *kv_prev)


def _sb_blocks(items, u_ref, Rs, accs):
    zs = [_dot(it["q"], it["kt"]) + it["bias"] for it in items]
    sps = [_softplus(z) if it["mask"] is None else jnp.where(it["mask"], _softplus(z), 0.0)
           for it, z in zip(items, zs)]
    parts = [_split(sp) for sp in sps]
    S = [None] * len(items)
    for w in sorted({z.shape[1] for z in zs}):
        idx = [n for n, z in enumerate(zs) if z.shape[1] == w]
        stack = jnp.concatenate([jnp.concatenate(parts[n], axis=1) for n in idx], axis=0)
        uu = jnp.concatenate([u_ref[0:w, 0:w]] * 2, axis=0)
        out = _dot(stack, uu)
        o = 0
        for n in idx:
            nr = zs[n].shape[0]
            S[n] = out[o:o + nr]
            o += nr
    Rs, accs = list(Rs), list(accs)
    for n, it in enumerate(items):
        r = it["row"]
        logw = zs[n] - S[n] - Rs[r]
        if it["mask"] is not None:
            logw = jnp.where(it["mask"], logw, -1e30)
        accs[r] = accs[r] + _dot_nt(jnp.exp(logw).astype(BF16), it["vt"])
        Rs[r] = Rs[r] + S[n][:, 0:1]
    return Rs, accs


def _sb_prompt_kernel(qi_ref, kj_ref, bias_ref, q_ref, kt_ref, vt_ref, g_ref, u_ref, o_ref, r_scr, acc_scr,
                      *, H, Dh, T, SUB, KC):
    p = pl.program_id(1)
    i, j = qi_ref[p], kj_ref[p]
    nsub = T // SUB
    HG = SB_HEADS_PER_ITER

    def run(diag):
        def head_group(hg, carry):
            rs = [slice(r * SUB, (r + 1) * SUB) for r in range(nsub)]
            items, Rs, accs = [], [], []
            for hh in range(HG):
                h = hg * HG + hh
                bias = bias_ref[h]
                hs = pl.ds(pl.multiple_of(h * Dh, Dh), Dh)
                for r in range(nsub):
                    if diag:
                        Rs.append(jnp.zeros((SUB, 1), F32))
                        accs.append(jnp.zeros((SUB, Dh), F32))
                    else:
                        Rs.append(r_scr[h, rs[r], :])
                        accs.append(acc_scr[h, rs[r], :])
                    q = q_ref[h, rs[r], :]
                    c1 = (r + 1) * SUB if diag else T
                    while c1 > 0:
                        c0 = ((c1 - 1) // KC) * KC
                        mask = None
                        if diag and c1 > r * SUB:
                            rows = lax.broadcasted_iota(jnp.int32, (SUB, c1 - c0), 0) + r * SUB
                            cols = lax.broadcasted_iota(jnp.int32, (SUB, c1 - c0), 1) + c0
                            mask = cols < rows
                        items.append(dict(row=hh * nsub + r, q=q, kt=kt_ref[hs, c0:c1].astype(BF16),
                                          vt=vt_ref[hs, c0:c1].astype(BF16), bias=bias, mask=mask))
                        c1 = c0
            Rs, accs = _sb_blocks(items, u_ref, Rs, accs)
            for hh in range(HG):
                h = hg * HG + hh
                for r in range(nsub):
                    r_scr[h, rs[r], :] = Rs[hh * nsub + r]
                    acc_scr[h, rs[r], :] = accs[hh * nsub + r]
            return carry

        lax.fori_loop(0, H // HG, head_group, 0)

    @pl.when(i == j)
    def _():
        run(True)

    @pl.when(i != j)
    def _():
        run(False)

    @pl.when(j == 0)
    def _():
        def normed(h):
            o = acc_scr[h]
            ms = jnp.mean(o * o, axis=-1, keepdims=True)
            return o * lax.rsqrt(ms + RMS_EPS) * g_ref[h]

        for hp in range(H // 2):
            o_ref[:, 2 * hp * Dh:(2 * hp + 2) * Dh] = jnp.concatenate(
                [normed(2 * hp), normed(2 * hp + 1)], axis=-1).astype(BF16)


def _sb_prompt(q_hm, kt, vt, bias, g_hm, seq0=0):
    H, M, Dh = q_hm.shape
    _, SW, L = kt.shape
    B = M // L
    T = _tile(L, SB_TILE)
    SUB = _tile(T, SB_SUB)
    KC = _tile(T, SB_KEY_CHUNK)
    nq = L // T
    assert H % SB_HEADS_PER_ITER == 0
    pairs = [(i, j) for i in range(nq) for j in range(i, -1, -1)]
    qi = jnp.array([p[0] for p in pairs], jnp.int32)
    kj = jnp.array([p[1] for p in pairs], jnp.int32)
    u = (jnp.arange(KC)[:, None] >= jnp.arange(KC)[None, :]).astype(BF16)
    kv_spec = pl.BlockSpec((None, SW, T), lambda b, p, qi, kj: (seq0 + b, 0, kj[p]))
    grid_spec = pltpu.PrefetchScalarGridSpec(
        num_scalar_prefetch=2,
        grid=(B, len(pairs)),
        in_specs=[
            pl.BlockSpec(memory_space=pltpu.SMEM),
            pl.BlockSpec((H, T, Dh), lambda b, p, qi, kj: (0, b * nq + qi[p], 0)),
            kv_spec, kv_spec,
            pl.BlockSpec((H, 1, Dh), lambda b, p, qi, kj: (0, 0, 0)),
            pl.BlockSpec((KC, KC), lambda b, p, qi, kj: (0, 0)),
        ],
        out_specs=pl.BlockSpec((T, SW), lambda b, p, qi, kj: (b * nq + qi[p], 0)),
        scratch_shapes=[pltpu.VMEM((H, T, 1), F32), pltpu.VMEM((H, T, Dh), F32)],
    )
    return pl.pallas_call(
        functools.partial(_sb_prompt_kernel, H=H, Dh=Dh, T=T, SUB=SUB, KC=KC),
        out_shape=jax.ShapeDtypeStruct((M, SW), BF16),
        grid_spec=grid_spec,
        compiler_params=_cparams(("parallel", "arbitrary")),
        name="sb_prompt",
    )(qi, kj, bias, q_hm, kt, vt, g_hm, u)


def _sb_decode_kernel(pt_ref, qbd_ref, kn_ref, vn_ref, bias_ref, g_ref, own_ref, u_ref, ck_hbm, cv_hbm,
                      o_ref, kbuf, vbuf, sem, acc_scr, r_scr, *, layer, G, Ld, H, Dh):
    b, s = pl.program_id(0), pl.program_id(1)
    nb, ns = pl.num_programs(0), pl.num_programs(1)
    NP = ns * G
    HQ = Ld * H
    P = u_ref.shape[0]
    t = b * ns + s
    slot = lax.rem(t, 2)

    def page_copies(bb, ss, sl):
        cps = []
        for i in range(G):
            page = pt_ref[bb, NP - 1 - (ss * G + i)]
            cps.append(pltpu.make_async_copy(ck_hbm.at[layer, page], kbuf.at[sl, i], sem.at[sl]))
            cps.append(pltpu.make_async_copy(cv_hbm.at[layer, page], vbuf.at[sl, i], sem.at[sl]))
        return cps

    @pl.when(t == 0)
    def _():
        for cp in page_copies(b, s, slot):
            cp.start()

    @pl.when(t + 1 < nb * ns)
    def _():
        wrap = s + 1 == ns
        for cp in page_copies(jnp.where(wrap, b + 1, b), jnp.where(wrap, 0, s + 1), 1 - slot):
            cp.start()

    for cp in page_copies(b, s, slot):
        cp.wait()

    q = qbd_ref[...]
    bias = bias_ref[...]

    @pl.when(s == 0)
    def _():
        rows = lax.broadcasted_iota(jnp.int32, (HQ, P), 0)
        cols = lax.broadcasted_iota(jnp.int32, (HQ, P), 1)
        mask = rows >= cols * H + H
        Rs, accs = _sb_blocks([dict(row=0, q=q, kt=kn_ref[...], vt=vn_ref[...], bias=bias, mask=mask)], u_ref,
                              [jnp.zeros((HQ, 1), F32)], [jnp.zeros((HQ, H * Dh), F32)])
        r_scr[...] = Rs[0]
        acc_scr[...] = accs[0]

    items = [dict(row=0, q=q, kt=kbuf[slot, i].astype(BF16), vt=vbuf[slot, i].astype(BF16), bias=bias, mask=None)
             for i in range(G)]
    Rs, accs = _sb_blocks(items, u_ref, [r_scr[...]], [acc_scr[...]])
    r_scr[...] = Rs[0]
    acc_scr[...] = accs[0]

    @pl.when(s == pl.num_programs(1) - 1)
    def _():
        o = acc_scr[...] * own_ref[...]
        ms = jnp.sum(o * o, axis=-1, keepdims=True) * (1.0 / Dh)
        o = o * lax.rsqrt(ms + RMS_EPS)
        g = g_ref[...]
        for t in range(Ld):
            o_ref[t:t + 1, :] = jnp.sum(o[t * H:(t + 1) * H, :], axis=0, keepdims=True) * g


def _sb_decode(layer, qbd, kt_new, vt_new, bias_col, g, cache_kt, cache_vt, page_table, Ld, H, Dh):
    Bd, HQ, SW = qbd.shape
    NP = page_table.shape[1]
    P = cache_kt.shape[3]
    G = DEC_PAGES_PER_STEP
    while NP % G:
        G //= 2
    u = (jnp.arange(P)[:, None] >= jnp.arange(P)[None, :]).astype(BF16)
    own = (jnp.arange(HQ)[:, None] % H == jnp.arange(SW)[None, :] // Dh).astype(F32)

    per_b = lambda r, c: pl.BlockSpec((None, r, c), lambda b, s, pt: (b, 0, 0))
    const = lambda a: pl.BlockSpec(a.shape, lambda b, s, pt: (0,) * a.ndim)
    hbm = pl.BlockSpec(memory_space=pl.ANY)
    grid_spec = pltpu.PrefetchScalarGridSpec(
        num_scalar_prefetch=1,
        grid=(Bd, NP // G),
        in_specs=[per_b(HQ, SW), per_b(SW, P), per_b(SW, P), const(bias_col), const(g), const(own), const(u),
                  hbm, hbm],
        out_specs=per_b(Ld, SW),
        scratch_shapes=[pltpu.VMEM((2, G, SW, P), F32), pltpu.VMEM((2, G, SW, P), F32),
                        pltpu.SemaphoreType.DMA((2,)),
                        pltpu.VMEM((HQ, SW), F32), pltpu.VMEM((HQ, 1), F32)],
    )
    return pl.pallas_call(
        functools.partial(_sb_decode_kernel, layer=layer, G=G, Ld=Ld, H=H, Dh=Dh),
        out_shape=jax.ShapeDtypeStruct((Bd, Ld, SW), F32),
        grid_spec=grid_spec,
        compiler_params=_cparams(("arbitrary", "arbitrary")),
        name="sb_decode",
    )(page_table, qbd, kt_new, vt_new, bias_col, g, own, u, cache_kt, cache_vt)


def _gla_kernel(q_ref, k_ref, lg_ref, v_ref, r_ref, s0_ref, g_ref, ltri2_ref, cmask_ref, o_ref, sout_ref, st_scr,
                *, C, nC, HPS):
    c = pl.program_id(2)

    @pl.when(c == 0)
    def _():
        st_scr[...] = s0_ref[...]

    cs = [slice(ci * C, (ci + 1) * C) for ci in range(nC)]
    ltri2 = ltri2_ref[...]
    DV = st_scr.shape[1]
    R = C * nC

    def epilogue(o, rows, vs):
        ms = jnp.mean(o * o, axis=-1, keepdims=True)
        o_ref[rows, vs] = (o * lax.rsqrt(ms + RMS_EPS) * g_ref[:, vs] * _silu(r_ref[rows, vs])).astype(BF16)

    bs = []
    for hh in range(HPS):
        hi, lo = _split(lg_ref[hh])
        bs.append(jnp.concatenate([_dot(ltri2, jnp.concatenate([hi[sl], lo[sl]], axis=0)) for sl in cs], axis=0))
    lowest = bs[0]
    for b in bs[1:]:
        lowest = jnp.minimum(lowest, b)
    safe = jnp.min(lowest) > GLA_MIN_CHUNK_LOG_GATE

    @pl.when(safe)
    def _():
        causal = cmask_ref[...] > 0.5
        for hh in range(HPS):
            vs = slice(hh * DV, (hh + 1) * DV)
            q, k, b = q_ref[hh], k_ref[hh], bs[hh]
            vb = v_ref[:, vs].astype(BF16)
            b_last = [b[(ci + 1) * C - 1:(ci + 1) * C, :] for ci in range(nC)]
            b_end = jnp.concatenate([jnp.broadcast_to(bl, (C, bl.shape[1])) for bl in b_last], axis=0)
            a = (q * jnp.exp(b)).astype(BF16)
            kd = (k * jnp.exp(-b)).astype(BF16)
            k2 = (k * jnp.exp(b_end - b)).astype(BF16)
            scs = [jnp.where(causal, _dot_nt(a[sl], kd[sl]), 0.0).astype(BF16) for sl in cs]
            o_intra = jnp.concatenate([_dot(sc, vb[sl]) for sc, sl in zip(scs, cs)], axis=0)
            upd = [_dot_tn(vb[sl], k2[sl]) for sl in cs]
            st = st_scr[hh]
            o_inter = []
            for ci in range(nC):
                o_inter.append(_dot_nt(a[cs[ci]], st.astype(BF16)))
                st = st * jnp.exp(b_last[ci]) + upd[ci]
            st_scr[hh] = st
            epilogue(o_intra + jnp.concatenate(o_inter, axis=0), slice(0, R), vs)

    @pl.when(jnp.logical_not(safe))
    def _():
        W = GLA_SLOW_ROWS
        rid = lax.broadcasted_iota(jnp.int32, (W, 1), 0)
        for hh in range(HPS):
            vs = slice(hh * DV, (hh + 1) * DV)

            def slab(t, st):
                rows = pl.ds(pl.multiple_of(t * W, W), W)
                q8, k8, g8 = q_ref[hh, rows, :], k_ref[hh, rows, :], lg_ref[hh, rows, :]
                v8 = v_ref[rows, vs]
                o8 = jnp.zeros((W, DV), F32)
                for j in range(W):
                    sel = rid == j
                    gj = jnp.sum(jnp.where(sel, g8, 0.0), axis=0, keepdims=True)
                    kj = jnp.where(sel, k8, 0.0).astype(BF16)
                    vj = jnp.where(sel, v8, 0.0).astype(BF16)
                    qj = jnp.where(sel, q8, 0.0).astype(BF16)
                    st = st * jnp.exp(gj) + _dot_tn(vj, kj)
                    o8 = o8 + _dot_nt(qj, st.astype(BF16))
                epilogue(o8, rows, vs)
                return st

            st_scr[hh] = lax.fori_loop(0, R // W, slab, st_scr[hh])

    @pl.when(c == pl.num_programs(2) - 1)
    def _():
        sout_ref[...] = st_scr[...]


def _gla(gq, gk, lg, gv, gr, s0t, g, B, L, C, nC):
    GH, M, DK = gq.shape
    DV = gv.shape[1] // GH
    HPS = GLA_HEADS_PER_STEP if GH % GLA_HEADS_PER_STEP == 0 else 1
    blk = C * nC
    nblk = L // blk
    t = jnp.arange(C)
    causal = t[:, None] >= t[None, :]
    ltri2 = jnp.concatenate([causal, causal], axis=1).astype(BF16)
    cmask = causal.astype(F32)
    hm = pl.BlockSpec((HPS, blk, DK), lambda b, h, c: (h, b * nblk + c, 0))
    tokm = pl.BlockSpec((blk, HPS * DV), lambda b, h, c: (b * nblk + c, h))
    state = pl.BlockSpec((None, HPS, DV, DK), lambda b, h, c: (b, h, 0, 0))
    const = lambda a: pl.BlockSpec(a.shape, lambda b, h, c: (0,) * a.ndim)
    return pl.pallas_call(
        functools.partial(_gla_kernel, C=C, nC=nC, HPS=HPS),
        out_shape=(jax.ShapeDtypeStruct((M, GH * DV), BF16), jax.ShapeDtypeStruct((B, GH, DV, DK), F32)),
        grid=(B, GH // HPS, nblk),
        in_specs=[hm, hm, hm, tokm, tokm, state,
                  pl.BlockSpec((1, HPS * DV), lambda b, h, c: (0, h)), const(ltri2), const(cmask)],
        out_specs=(tokm, state),
        scratch_shapes=[pltpu.VMEM((HPS, DV, DK), F32)],
        compiler_params=_cparams(("parallel", "parallel", "arbitrary")),
        name="gla",
    )(gq, gk, lg, gv, gr, s0t, g, ltri2, cmask)


def _merge_ln_kernel(a_ref, b_ref, x_ref, wa_ref, wb_ref, g_ref, beta_ref, o_ref, *, alpha):
    y = _dot(a_ref[...], wa_ref[...]) + _dot(b_ref[...], wb_ref[...])
    o_ref[...] = _layer_norm(alpha * x_ref[...] + y, g_ref[...], beta_ref[...])


def _merge_ln(a, b, x, wa, wb, g, beta, alpha):
    M, D = x.shape
    tm = _tile(M, ROW_TILE)
    row = lambda n: pl.BlockSpec((tm, n), lambda i: (i, 0))
    full = lambda t: pl.BlockSpec(t.shape, lambda i: (0,) * t.ndim)
    return pl.pallas_call(
        functools.partial(_merge_ln_kernel, alpha=alpha),
        out_shape=jax.ShapeDtypeStruct((M, D), F32),
        grid=(M // tm,),
        in_specs=[row(a.shape[1]), row(b.shape[1]), row(D), full(wa), full(wb), full(g), full(beta)],
        out_specs=row(D),
        compiler_params=_cparams(("parallel",)),
        name="merge_ln",
    )(a, b, x, wa, wb, g, beta)


def _ffn_kernel(te_ref, x_ref, wg_ref, wu_ref, wd_ref, *rest, nf, tf, n_tiles, alpha, fuse_ln):
    if fuse_ln:
        g_ref, beta_ref, o_ref, *scratch = rest
    else:
        o_ref, *scratch = rest
    if nf > 1:
        xb_scr, acc_scr = scratch
    i = pl.program_id(0)

    @pl.when(i < te_ref[n_tiles])
    def _():
        xb = x_ref[...].astype(BF16)
        if nf == 1:
            y = _dot((_silu(_dot(xb, wg_ref[...])) * _dot(xb, wu_ref[...])).astype(BF16), wd_ref[...])
        else:
            xb_scr[...] = xb
            acc_scr[...] = jnp.zeros_like(acc_scr)

            def body(f, carry):
                xc = xb_scr[...]
                cols = pl.ds(pl.multiple_of(f * tf, tf), tf)
                hg = _dot(xc, wg_ref[:, cols])
                hu = _dot(xc, wu_ref[:, cols])
                acc_scr[...] += _dot((_silu(hg) * hu).astype(BF16), wd_ref[cols, :])
                return carry

            lax.fori_loop(0, nf, body, 0)
            y = acc_scr[...]
        if fuse_ln:
            o_ref[...] = _layer_norm(alpha * x_ref[...] + y, g_ref[...], beta_ref[...])
        else:
            o_ref[...] = y

    @pl.when(i >= te_ref[n_tiles])
    def _():
        o_ref[...] = jnp.zeros_like(o_ref)


def _ffn(x, te, wg, wu, wd, tm, ln=None, alpha=1.0):
    R, D = x.shape
    F = wg.shape[2]
    tf = _tile(F, FF_CHUNK)
    nf = F // tf
    n_tiles = R // tm
    row = pl.BlockSpec((tm, D), lambda i, te: (i, 0))
    w_in = pl.BlockSpec((None, D, F), lambda i, te: (te[i], 0, 0))
    w_out = pl.BlockSpec((None, F, D), lambda i, te: (te[i], 0, 0))
    vec = pl.BlockSpec((1, D), lambda i, te: (0, 0))
    in_specs = [row, w_in, w_in, w_out] + ([vec, vec] if ln else [])
    grid_spec = pltpu.PrefetchScalarGridSpec(
        num_scalar_prefetch=1, grid=(n_tiles,), in_specs=in_specs, out_specs=row,
        scratch_shapes=[pltpu.VMEM((tm, D), BF16), pltpu.VMEM((tm, D), F32)] if nf > 1 else [])
    return pl.pallas_call(
        functools.partial(_ffn_kernel, nf=nf, tf=tf, n_tiles=n_tiles, alpha=alpha, fuse_ln=bool(ln)),
        out_shape=jax.ShapeDtypeStruct((R, D), F32),
        grid_spec=grid_spec,
        compiler_params=_cparams(("arbitrary",)),
        name="ffn_ln" if ln else "ffn_experts",
    )(te, x, wg, wu, wd, *(ln or ()))


def _router_kernel(x_ref, wh_ref, wl_ref, b_ref, c0_ref, lstrict_ref, slab_ref, cnt_ref, carry_scr, *, E):
    i = pl.program_id(0)

    @pl.when(i == 0)
    def _():
        carry_scr[...] = c0_ref[...]

    xh, xl = _split(x_ref[...])
    wh, wl = wh_ref[...], wl_ref[...]
    logits = _dot(xh, wh) + (_dot(xl, wh) + _dot(xh, wl)) + b_ref[...]
    tm, W = logits.shape
    lane = lax.broadcasted_iota(jnp.int32, (tm, W), 1).astype(F32)
    neg = -jnp.inf
    lg1 = jnp.where(lane < E, logits, neg)
    m1 = jnp.max(lg1, axis=-1, keepdims=True)
    i1 = jnp.min(jnp.where(lg1 == m1, lane, float(W)), axis=-1, keepdims=True)
    lg2 = jnp.where(lane == i1, neg, lg1)
    m2 = jnp.max(lg2, axis=-1, keepdims=True)
    i2 = jnp.min(jnp.where(lg2 == m2, lane, float(W)), axis=-1, keepdims=True)
    e2 = jnp.exp(m2 - m1)
    g1 = 1.0 / (1.0 + e2)
    g2 = e2 / (1.0 + e2)
    oh = jnp.where((lane == i1) | (lane == i2), 1.0, 0.0)
    before = _dot(lstrict_ref[...], oh.astype(BF16)) + carry_scr[...]
    r1 = jnp.sum(jnp.where(lane == i1, before, 0.0), axis=-1, keepdims=True)
    r2 = jnp.sum(jnp.where(lane == i2, before, 0.0), axis=-1, keepdims=True)
    carry_scr[...] += jnp.sum(oh, axis=0, keepdims=True)
    slab = jnp.zeros((tm, W), F32)
    for col, val in enumerate((i1, i2, g1, g2, r1, r2)):
        slab = jnp.where(lane == col, val, slab)
    slab_ref[...] = slab
    cnt_ref[...] = carry_scr[...]


def _router(x, wh, wl, b, c0, E):
    M, D = x.shape
    tm = _tile(M, ROW_TILE)
    W = wh.shape[1]
    lstrict = (jnp.arange(tm)[:, None] > jnp.arange(tm)[None, :]).astype(BF16)
    full = lambda t: pl.BlockSpec(t.shape, lambda i: (0,) * t.ndim)
    return pl.pallas_call(
        functools.partial(_router_kernel, E=E),
        out_shape=(jax.ShapeDtypeStruct((M, W), F32), jax.ShapeDtypeStruct((1, W), F32)),
        grid=(M // tm,),
        in_specs=[pl.BlockSpec((tm, D), lambda i: (i, 0)), full(wh), full(wl), full(b), full(c0), full(lstrict)],
        out_specs=(pl.BlockSpec((tm, W), lambda i: (i, 0)), pl.BlockSpec((1, W), lambda i: (0, 0))),
        scratch_shapes=[pltpu.VMEM((1, W), F32)],
        compiler_params=_cparams(("arbitrary",)),
        name="router",
    )(x, wh, wl, b, c0, lstrict)


def _row_copy(src, s, dst, d, sem):
    return pltpu.make_async_copy(src.at[pl.ds(s, 1)], dst.at[pl.ds(d, 1)], sem)


def _dispatch_kernel(dest_ref, x_ref, xs_in, xs_out, sem, *, tm):
    del xs_in

    def start(r, carry):
        for j in range(TOP_K):
            _row_copy(x_ref, r, xs_out, dest_ref[0, 0, TOP_K * r + j], sem).start()
        return carry

    lax.fori_loop(0, tm, start, 0, unroll=ROW_DMA_UNROLL)
    for j in range(TOP_K):
        pltpu.make_async_copy(x_ref, xs_out.at[pl.ds(0, tm)], sem).wait()


def _dispatch(x, dest3, xs):
    M, D = x.shape
    tm = dest3.shape[2] // TOP_K
    return pl.pallas_call(
        functools.partial(_dispatch_kernel, tm=tm),
        out_shape=jax.ShapeDtypeStruct(xs.shape, xs.dtype),
        grid=(M // tm,),
        in_specs=[pl.BlockSpec((1, 1, TOP_K * tm), lambda i: (i, 0, 0), memory_space=pltpu.SMEM),
                  pl.BlockSpec((tm, D), lambda i: (i, 0)), pl.BlockSpec(memory_space=pl.ANY)],
        out_specs=pl.BlockSpec(memory_space=pl.ANY),
        scratch_shapes=[pltpu.SemaphoreType.DMA],
        input_output_aliases={2: 0},
        compiler_params=_cparams(("arbitrary",)),
        name="moe_dispatch",
    )(dest3, x, xs)


def _combine_kernel(dest_ref, slab_ref, x_ref, g_ref, beta_ref, ys_hbm, o_ref, buf, sem, *, tm, alpha):
    def start(r, carry):
        for j in range(TOP_K):
            _row_copy(ys_hbm, dest_ref[0, 0, TOP_K * r + j], buf.at[j], r, sem).start()
        return carry

    lax.fori_loop(0, tm, start, 0, unroll=ROW_DMA_UNROLL)
    for j in range(TOP_K):
        pltpu.make_async_copy(ys_hbm.at[pl.ds(0, tm)], buf.at[j], sem).wait()
    slab = slab_ref[...]
    y = slab[:, 2:3] * buf[0] + slab[:, 3:4] * buf[1]
    o_ref[...] = _layer_norm(alpha * x_ref[...] + y, g_ref[...], beta_ref[...])


def _combine(x, slab, dest3, ys, g, beta, alpha):
    M, D = x.shape
    tm = dest3.shape[2] // TOP_K
    row = lambda n: pl.BlockSpec((tm, n), lambda i: (i, 0))
    vec = pl.BlockSpec((1, D), lambda i: (0, 0))
    return pl.pallas_call(
        functools.partial(_combine_kernel, tm=tm, alpha=alpha),
        out_shape=jax.ShapeDtypeStruct((M, D), F32),
        grid=(M // tm,),
        in_specs=[pl.BlockSpec((1, 1, TOP_K * tm), lambda i: (i, 0, 0), memory_space=pltpu.SMEM),
                  row(slab.shape[1]), row(D), vec, vec, pl.BlockSpec(memory_space=pl.ANY)],
        out_specs=row(D),
        scratch_shapes=[pltpu.VMEM((TOP_K, tm, D), F32), pltpu.SemaphoreType.DMA],
        compiler_params=_cparams(("arbitrary",)),
        name="moe_combine",
    )(dest3, slab, x, g, beta, ys)


def _ffn_weights(wg, wu, wd):
    return wg.astype(BF16), wu.astype(BF16), wd.astype(BF16)


def _moe(xp, xs, router_w, router_b, wg, wu, wd, ln_g, ln_b, alpha):
    E = router_w.shape[1]
    D = xp.shape[1]
    pad = ROUTER_PAD - E
    rw = jnp.pad(router_w, ((0, 0), (0, pad)))
    rwh = rw.astype(BF16)
    rwl = (rw - rwh.astype(F32)).astype(BF16)
    rb = jnp.pad(router_b, (0, pad))[None, :]
    slab_p, cnt_p = _router(xp, rwh, rwl, rb, jnp.zeros((1, ROUTER_PAD), F32), E)
    slab_s, cnt = _router(xs, rwh, rwl, rb, cnt_p, E)

    n_assign = TOP_K * (xp.shape[0] + xs.shape[0])
    tm = ROW_TILE
    n_tiles = -(-n_assign // tm) + E
    counts = cnt[0, :E].astype(jnp.int32)
    padded = ((counts + tm - 1) // tm) * tm
    off_end = jnp.cumsum(padded)
    off_start = off_end - padded
    tile_row0 = jnp.arange(n_tiles, dtype=jnp.int32) * tm
    tile_expert = jnp.minimum(jnp.sum(tile_row0[:, None] >= off_end[None, :], axis=1), E - 1)
    te = jnp.concatenate([tile_expert, off_end[-1:] // tm]).astype(jnp.int32)

    def dests(slab):
        e = slab[:, 0:TOP_K].astype(jnp.int32)
        rank = slab[:, 4:4 + TOP_K].astype(jnp.int32)
        d = off_start[e] + rank
        t = _tile(slab.shape[0], ROW_TILE)
        return d.reshape(slab.shape[0] // t, 1, TOP_K * t)

    dest_p, dest_s = dests(slab_p), dests(slab_s)
    xsort = jnp.zeros((n_tiles * tm, D), F32)
    xsort = _dispatch(xp, dest_p, xsort)
    xsort = _dispatch(xs, dest_s, xsort)
    ys = _ffn(xsort, te, wg, wu, wd, tm)
    return (_combine(xp, slab_p, dest_p, ys, ln_g, ln_b, alpha),
            _combine(xs, slab_s, dest_s, ys, ln_g, ln_b, alpha))


def kernel(x_prompt, x_sample, cache_k, cache_v, state_gla, page_table, w_in, w_gate_up, b_gate, sb_bias,
           sb_norm_g, gla_norm_g, w_o, ln1_g, ln1_b, ln2_g, ln2_b, ffn_w_gate, ffn_w_up, ffn_w_down,
           router_w, router_b, moe_w_gate, moe_w_up, moe_w_down):
    B, L, D = x_prompt.shape
    Bd, Ld, _ = x_sample.shape
    depth, n_pool, P, H, Dh = cache_k.shape
    _, _, GH, DK, DV = state_gla.shape
    assert Dh == 64 and DK == 64, "1/sqrt(head dim) must be a power of two to fold into bf16 operands"
    SW, KW, GW = H * Dh, GH * DK, GH * DV
    n_main = 3 * SW + 2 * KW + 2 * GW
    rank = w_gate_up.shape[1]
    alpha = (2 * depth) ** 0.25
    dims = (H, Dh, GH, DK, DV)
    Ls = -(-Ld // GLA_SAMPLE_CHUNK) * GLA_SAMPLE_CHUNK

    xp = x_prompt.reshape(B * L, D)
    xs = x_sample.reshape(Bd * Ld, D)
    cache_kt = cache_k.transpose(0, 1, 3, 4, 2).reshape(depth, n_pool, SW, P)
    cache_vt = cache_v.transpose(0, 1, 3, 4, 2).reshape(depth, n_pool, SW, P)
    head_eye = jnp.eye(H, dtype=BF16)
    row2 = lambda v: v[None, :]
    outs = [[] for _ in range(6)]
    kv_prompt = (jnp.zeros((depth * B, SW, L), F32), jnp.zeros((depth * B, SW, L), F32))

    for l in range(depth):
        w_main = jnp.concatenate([w_in[l, :, :SW], w_in[l, :, 3 * SW:n_main]], axis=1).astype(BF16)
        w_kvt = w_in[l, :, SW:3 * SW].T.astype(BF16)
        w_gd = jnp.pad(w_in[l, :, n_main:], ((0, 0), (0, GLA_GATE_PAD - rank))).astype(BF16)
        w_up = jnp.pad(w_gate_up[l], ((0, GLA_GATE_PAD - rank), (0, 0))).astype(BF16)
        bg = row2(b_gate[l])
        wo = w_o[l].astype(BF16)
        g_sb_hm = sb_norm_g[l].reshape(H, 1, Dh)
        g_gla = row2(gla_norm_g[l])

        q, kt, vt, gq, gk, lg, gv, gr = _in_proj(xp, w_main, w_kvt, w_gd, w_up, bg, dims, seq_len=L,
                                                 layer=l, depth=depth, kv_prev=kv_prompt)
        kv_prompt = (kt, vt)
        a = _sb_prompt(q, kt, vt, sb_bias[l], g_sb_hm, seq0=l * B)
        C = _tile(L, GLA_CHUNK)
        nC = GLA_CHUNKS_PER_STEP
        while (L // C) % nC:
            nC //= 2
        bo, stp = _gla(gq, gk, lg, gv, gr, jnp.zeros((B, GH, DV, DK), F32), g_gla, B, L, C, nC)
        xp = _merge_ln(a, bo, xp, wo[:SW], wo[SW:], row2(ln1_g[l]), row2(ln1_b[l]), alpha)
        outs[2].append(jnp.swapaxes(stp, -1, -2))

        q, kf, vf, gq, gk, lg, gv, gr = _in_proj(xs, w_main, w_kvt, w_gd, w_up, bg, dims)
        q4 = q.reshape(H, Bd, Ld, Dh).transpose(1, 2, 0, 3)
        qbd = (q4[:, :, :, None, :] * head_eye[None, None, :, :, None]).reshape(Bd, Ld * H, SW)
        pad_new = lambda t: jnp.pad(jnp.swapaxes(t.reshape(Bd, Ld, SW), 1, 2).astype(BF16),
                                    ((0, 0), (0, 0), (0, P - Ld)))
        bias_col = jnp.tile(sb_bias[l], Ld)[:, None]
        a = _sb_decode(l, qbd, pad_new(kf), pad_new(vf), bias_col, row2(sb_norm_g[l]),
                       cache_kt, cache_vt, page_table, Ld, H, Dh)
        a = a.reshape(Bd * Ld, SW).astype(BF16)
        pad_hm = lambda t: jnp.pad(t.reshape(GH, Bd, Ld, DK), ((0, 0), (0, 0), (0, Ls - Ld), (0, 0))
                                   ).reshape(GH, Bd * Ls, DK)
        pad_tm = lambda t: jnp.pad(t.reshape(Bd, Ld, GW), ((0, 0), (0, Ls - Ld), (0, 0))).reshape(Bd * Ls, GW)
        bo, sts = _gla(pad_hm(gq), pad_hm(gk), pad_hm(lg), pad_tm(gv), pad_tm(gr),
                       jnp.swapaxes(state_gla[l], -1, -2), g_gla, Bd, Ls, Ls, 1)
        bo = bo.reshape(Bd, Ls, GW)[:, :Ld].reshape(Bd * Ld, GW)
        xs = _merge_ln(a, bo, xs, wo[:SW], wo[SW:], row2(ln1_g[l]), row2(ln1_b[l]), alpha)
        outs[3].append(kf.reshape(Bd, Ld, H, Dh))
        outs[4].append(vf.reshape(Bd, Ld, H, Dh))
        outs[5].append(jnp.swapaxes(sts, -1, -2))

        i = l // 2
        ln = (row2(ln2_g[l]), row2(ln2_b[l]))
        if l % 2 == 0:
            wg, wu, wd = _ffn_weights(ffn_w_gate[i][None], ffn_w_up[i][None], ffn_w_down[i][None])
            for_x = lambda x: _ffn(x, jnp.array([0] * (x.shape[0] // _tile(x.shape[0], ROW_TILE))
                                                + [x.shape[0] // _tile(x.shape[0], ROW_TILE)], jnp.int32),
                                   wg, wu, wd, _tile(x.shape[0], ROW_TILE), ln=ln, alpha=alpha)
            xp, xs = for_x(xp), for_x(xs)
        else:
            wg, wu, wd = _ffn_weights(moe_w_gate[i], moe_w_up[i], moe_w_down[i])
            xp, xs = _moe(xp, xs, router_w[i], router_b[i], wg, wu, wd, ln[0], ln[1], alpha)

    stack = lambda rows: jnp.stack(rows, axis=0)
    kv_out = lambda t: t.reshape(depth, B, H, Dh, L).transpose(0, 1, 4, 2, 3)
    return (xp.reshape(B, L, D), xs.reshape(Bd, Ld, D), kv_out(kv_prompt[0]), kv_out(kv_prompt[1]), stack(outs[2]),
            stack(outs[3]), stack(outs[4]), stack(outs[5]))
```
